```python
import math
import jax, jax.numpy as jnp
from jax import lax
import numpy as np

D_MODEL = 2048
BATCH = 2
SEQ = 4096
DEPTH = 2

CHUNK = 64
Q_BLOCK = 128
NORM_EPS = 1e-6

A_HEADS = 8
A_HEAD_DIM = 128
A_WIDTH = A_HEADS * 2 * A_HEAD_DIM
B_GROUPS = 8
B_WIDTH = D_MODEL
B_GROUP_DIM = B_WIDTH // B_GROUPS
B_SPAN = 128
C_WIDTH = (4 * D_MODEL // 3) // 128 * 128
C_HEADS = 16
C_BLOCK = C_WIDTH // C_HEADS
C_CONV = 4
C_GATE_C = 8.0

N_EVEN = (DEPTH + 1) // 2
N_ODD = DEPTH // 2
AB_IN = 4 * A_WIDTH + 3 * B_WIDTH
AB_MIX = A_WIDTH + B_WIDTH

kernel_name = "chunk_causal_diffattn_gmlp_rglru_hybrid"


def rms_norm(x, g):
    xf = x.astype(jnp.float32)
    y = xf * lax.rsqrt(jnp.mean(xf * xf, axis=-1, keepdims=True) + NORM_EPS)
    return (y * g.astype(jnp.float32)).astype(x.dtype)


def layer_norm(x, g, b):
    xf = x.astype(jnp.float32)
    mu = jnp.mean(xf, axis=-1, keepdims=True)
    xc = xf - mu
    y = xc * lax.rsqrt(jnp.mean(xc * xc, axis=-1, keepdims=True) + NORM_EPS)
    return (y * g.astype(jnp.float32) + b.astype(jnp.float32)).astype(x.dtype)


def diff_attention(q, k, v, lam):
    bsz, seq = q.shape[0], q.shape[1]
    nb = seq // Q_BLOCK
    k1, k2 = k[..., 0, :], k[..., 1, :]
    key_chunk = jnp.arange(seq) // CHUNK
    qb = q.reshape(bsz, nb, Q_BLOCK, A_HEADS, 2, A_HEAD_DIM).transpose(1, 0, 2, 3, 4, 5)

    def block(args):
        qblk, bi = args
        q_chunk = (bi * Q_BLOCK + jnp.arange(Q_BLOCK)) // CHUNK
        mask = key_chunk[None, :] <= q_chunk[:, None]
        s1 = jnp.einsum('bqhd,bkhd->bhqk', qblk[..., 0, :], k1).astype(jnp.float32)
        s2 = jnp.einsum('bqhd,bkhd->bhqk', qblk[..., 1, :], k2).astype(jnp.float32)
        p1 = jax.nn.softmax(jnp.where(mask, s1, -jnp.inf), axis=-1)
        p2 = jax.nn.softmax(jnp.where(mask, s2, -jnp.inf), axis=-1)
        w = (p1 - lam * p2).astype(v.dtype)
        return jnp.einsum('bhqk,bkhe->bqhe', w, v)

    out = lax.map(block, (qb, jnp.arange(nb)))
    return out.transpose(1, 0, 2, 3, 4).reshape(bsz, seq, A_HEADS, 2 * A_HEAD_DIM)


def spatial_gating(u, v, ln_g, ln_b, w_s, b_s):
    bsz, seq = v.shape[0], v.shape[1]
    n = seq // B_SPAN
    v = layer_norm(v, ln_g, ln_b)
    vg = v.reshape(bsz, n, B_SPAN, B_GROUPS, B_GROUP_DIM)
    pos_chunk = jnp.arange(B_SPAN) // CHUNK
    mask = pos_chunk[:, None] >= pos_chunk[None, :]
    w = jnp.where(mask[None], w_s, jnp.zeros_like(w_s))
    mixed = jnp.einsum('gij,bnjgc->bnigc', w, vg) + b_s.T[None, None, :, :, None]
    return u * mixed.reshape(bsz, seq, B_WIDTH)


def causal_depthwise_conv(x, w, b):
    seq = x.shape[1]
    xp = jnp.pad(x, ((0, 0), (C_CONV - 1, 0), (0, 0)))
    y = xp[:, 0:seq] * w[0]
    for t in range(1, C_CONV):
        y = y + xp[:, t:t + seq] * w[t]
    return y + b


def rg_lru(x, w_a, b_a, w_x, b_x, lam):
    bsz, seq = x.shape[0], x.shape[1]
    xh = x.reshape(bsz, seq, C_HEADS, C_BLOCK)
    r = jax.nn.sigmoid((jnp.einsum('bshi,hij->bshj', xh, w_a).reshape(bsz, seq, C_WIDTH) + b_a).astype(jnp.float32))
    i = jax.nn.sigmoid((jnp.einsum('bshi,hij->bshj', xh, w_x).reshape(bsz, seq, C_WIDTH) + b_x).astype(jnp.float32))
    log_a = -C_GATE_C * r * jax.nn.softplus(-lam.astype(jnp.float32))
    a = jnp.exp(log_a)
    beta = jnp.sqrt(-jnp.expm1(2.0 * log_a))
    bt = beta * (i * x.astype(jnp.float32))

    def combine(lhs, rhs):
        a_l, b_l = lhs
        a_r, b_r = rhs
        return a_l * a_r, a_r * b_l + b_r

    _, h = lax.associative_scan(combine, (a, bt), axis=1)
    return h.astype(x.dtype)


def setup_inputs(seed: int = 0) -> dict:
    key = jax.random.key(seed)
    ks = jax.random.split(key, 24)
    f32 = jnp.float32
    nrm = lambda k, shape, s: (jax.random.normal(k, shape, f32) * s)
    x = jax.random.normal(ks[0], (BATCH, SEQ, D_MODEL), f32)
    ab_norm = 1.0 + nrm(ks[1], (N_EVEN, D_MODEL), 0.01)
    ab_w_in = nrm(ks[2], (N_EVEN, D_MODEL, AB_IN), D_MODEL ** -0.5)
    ab_lambda = nrm(ks[3], (N_EVEN, 4, A_HEAD_DIM), 0.1)
    ab_head_norm = 1.0 + nrm(ks[4], (N_EVEN, A_WIDTH), 0.01)
    ab_sgu_ln_g = 1.0 + nrm(ks[5], (N_EVEN, B_WIDTH), 0.01)
    ab_sgu_ln_b = nrm(ks[6], (N_EVEN, B_WIDTH), 0.01)
    ab_sgu_w = nrm(ks[7], (N_EVEN, B_GROUPS, B_SPAN, B_SPAN), B_SPAN ** -0.5)
    ab_sgu_b = 1.0 + nrm(ks[8], (N_EVEN, B_GROUPS, B_SPAN), 0.01)
    ab_w_out = nrm(ks[9], (N_EVEN, AB_MIX, D_MODEL), (AB_MIX * 2 * DEPTH) ** -0.5)
    c_norm = 1.0 + nrm(ks[10], (N_ODD, D_MODEL), 0.01)
    c_w_in = nrm(ks[11], (N_ODD, D_MODEL, 2 * C_WIDTH), D_MODEL ** -0.5)
    c_conv_w = nrm(ks[12], (N_ODD, C_CONV, C_WIDTH), C_CONV ** -0.5)
    c_conv_b = nrm(ks[13], (N_ODD, C_WIDTH), 0.01)
    c_gate_a_w = nrm(ks[14], (N_ODD, C_HEADS, C_BLOCK, C_BLOCK), C_BLOCK ** -0.5)
    c_gate_a_b = nrm(ks[15], (N_ODD, C_WIDTH), 0.01)
    c_gate_x_w = nrm(ks[16], (N_ODD, C_HEADS, C_BLOCK, C_BLOCK), C_BLOCK ** -0.5)
    c_gate_x_b = nrm(ks[17], (N_ODD, C_WIDTH), 0.01)
    a_pow_c = jax.random.uniform(ks[18], (N_ODD, C_WIDTH), f32, 0.9, 0.999)
    a0 = a_pow_c ** (1.0 / C_GATE_C)
    c_lambda = jnp.log(a0) - jnp.log1p(-a0)
    c_w_out = nrm(ks[19], (N_ODD, C_WIDTH, D_MODEL), (C_WIDTH * 2 * DEPTH) ** -0.5)
    final_norm = 1.0 + nrm(ks[20], (D_MODEL,), 0.01)
    return {"x": x, "ab_norm": ab_norm, "ab_w_in": ab_w_in, "ab_lambda": ab_lambda,
            "ab_head_norm": ab_head_norm, "ab_sgu_ln_g": ab_sgu_ln_g, "ab_sgu_ln_b": ab_sgu_ln_b,
            "ab_sgu_w": ab_sgu_w, "ab_sgu_b": ab_sgu_b, "ab_w_out": ab_w_out,
            "c_norm": c_norm, "c_w_in": c_w_in, "c_conv_w": c_conv_w, "c_conv_b": c_conv_b,
            "c_gate_a_w": c_gate_a_w, "c_gate_a_b": c_gate_a_b, "c_gate_x_w": c_gate_x_w,
            "c_gate_x_b": c_gate_x_b, "c_lambda": c_lambda, "c_w_out": c_w_out,
            "final_norm": final_norm}


def reference(x, ab_norm, ab_w_in, ab_lambda, ab_head_norm, ab_sgu_ln_g, ab_sgu_ln_b,
              ab_sgu_w, ab_sgu_b, ab_w_out, c_norm, c_w_in, c_conv_w, c_conv_b,
              c_gate_a_w, c_gate_a_b, c_gate_x_w, c_gate_x_b, c_lambda, c_w_out, final_norm):
    bsz, seq = x.shape[0], x.shape[1]
    for l in range(DEPTH):
        if l % 2 == 0:
            e = l // 2
            lam_init = 0.8 - 0.6 * math.exp(-0.3 * l)
            xn = rms_norm(x, ab_norm[e])
            proj = xn @ ab_w_in[e]
            q, k, v, z_a, u_b, v_b, z_b = jnp.split(
                proj, np.cumsum([A_WIDTH, A_WIDTH, A_WIDTH, A_WIDTH, B_WIDTH, B_WIDTH]), axis=-1)
            q = q.reshape(bsz, seq, A_HEADS, 2, A_HEAD_DIM) * (A_HEAD_DIM ** -0.5)
            k = k.reshape(bsz, seq, A_HEADS, 2, A_HEAD_DIM)
            v = v.reshape(bsz, seq, A_HEADS, 2 * A_HEAD_DIM)
            lp = ab_lambda[e].astype(jnp.float32)
            lam = jnp.exp(jnp.dot(lp[0], lp[1])) - jnp.exp(jnp.dot(lp[2], lp[3])) + lam_init
            attn = diff_attention(q, k, v, lam)
            attn = rms_norm(attn, ab_head_norm[e].reshape(A_HEADS, 2 * A_HEAD_DIM)) * (1.0 - lam_init)
            y_a = attn.reshape(bsz, seq, A_WIDTH) * jax.nn.silu(z_a)
            sgu = spatial_gating(jax.nn.gelu(u_b), jax.nn.gelu(v_b), ab_sgu_ln_g[e], ab_sgu_ln_b[e],
                                 ab_sgu_w[e], ab_sgu_b[e])
            y_b = sgu * jax.nn.silu(z_b)
            y = jnp.concatenate([y_a, y_b], axis=-1) @ ab_w_out[e]
        else:
            o = l // 2
            xn = rms_norm(x, c_norm[o])
            proj = xn @ c_w_in[o]
            xb, z_c = jnp.split(proj, 2, axis=-1)
            xb = causal_depthwise_conv(xb, c_conv_w[o], c_conv_b[o])
            h = rg_lru(xb, c_gate_a_w[o], c_gate_a_b[o], c_gate_x_w[o], c_gate_x_b[o], c_lambda[o])
            y = (h * jax.nn.silu(z_c)) @ c_w_out[o]
        x = x + y
    return rms_norm(x, final_norm)
```

```python
import functools
import math

import jax
import jax.numpy as jnp
import numpy as np
from jax import lax
from jax.experimental import pallas as pl
from jax.experimental.pallas import tpu as pltpu

D_MODEL = 2048
CHUNK = 64
NORM_EPS = 1e-6
A_HEADS = 8
A_HEAD_DIM = 128
A_VDIM = 2 * A_HEAD_DIM
A_WIDTH = A_HEADS * A_VDIM
B_GROUPS = 8
B_WIDTH = D_MODEL
B_GROUP_DIM = B_WIDTH // B_GROUPS
B_SPAN = 128
C_WIDTH = 2688
C_HEADS = 16
C_BLOCK = C_WIDTH // C_HEADS
C_CONV = 4
C_GATE_C = 8.0
AB_IN = 4 * A_WIDTH + 3 * B_WIDTH
LAM_INIT_0 = 0.8 - 0.6 * math.exp(-0.3 * 0)

LANES = 128
SUBLANES = 8
LOG2E = 1.4426950408889634
VMEM_LIMIT = 56 * 1024 * 1024

GATE_WIN = 512
N_CT = C_WIDTH // LANES


def _gate_window_start(j):
    h0 = (LANES * j) // C_BLOCK
    ws = (C_BLOCK * h0) // LANES * LANES
    return min(ws, C_WIDTH - GATE_WIN)


GATE_WS = tuple(_gate_window_start(j) for j in range(N_CT))


def _silu(z):
    return z * (1.0 / (1.0 + jnp.exp(-z)))


def _gelu_tanh(x):
    c = math.sqrt(2.0 / math.pi)
    return 0.5 * x * (1.0 + jnp.tanh(c * (x + 0.044715 * (x * x * x))))


def _rms_rows(xf, g):
    ms = jnp.mean(xf * xf, axis=-1, keepdims=True)
    return xf * lax.rsqrt(ms + NORM_EPS) * g


def _proj0_kernel(x_ref, g_ref, w_ref, o_ref, xn_sc, *, tiles_per_seg):
    j = pl.program_id(1)

    @pl.when(j == 0)
    def _():
        xn_sc[...] = _rms_rows(x_ref[...], g_ref[...]).astype(jnp.bfloat16)

    acc = jnp.dot(xn_sc[...], w_ref[...], preferred_element_type=jnp.float32)
    seg = j // tiles_per_seg

    @pl.when(seg == 0)
    def _():
        o_ref[...] = (acc * (A_HEAD_DIM ** -0.5 * LOG2E)).astype(o_ref.dtype)

    @pl.when((seg == 1) | (seg == 2))
    def _():
        o_ref[...] = acc.astype(o_ref.dtype)

    @pl.when((seg == 3) | (seg == 6))
    def _():
        o_ref[...] = _silu(acc).astype(o_ref.dtype)

    @pl.when((seg == 4) | (seg == 5))
    def _():
        o_ref[...] = _gelu_tanh(acc).astype(o_ref.dtype)


def _proj0(x2, g, w_bf, *, tm=1024, tn=512):
    t = x2.shape[0]
    n = w_bf.shape[1]
    return pl.pallas_call(
        functools.partial(_proj0_kernel, tiles_per_seg=A_WIDTH // tn),
        grid=(t // tm, n // tn),
        in_specs=[
            pl.BlockSpec((tm, D_MODEL), lambda i, j: (i, 0)),
            pl.BlockSpec((1, D_MODEL), lambda i, j: (0, 0)),
            pl.BlockSpec((D_MODEL, tn), lambda i, j: (0, j)),
        ],
        out_specs=pl.BlockSpec((tm, tn), lambda i, j: (i, j)),
        out_shape=jax.ShapeDtypeStruct((t, n), jnp.bfloat16),
        scratch_shapes=[pltpu.VMEM((tm, D_MODEL), jnp.bfloat16)],
        compiler_params=pltpu.CompilerParams(
            dimension_semantics=("arbitrary", "arbitrary"),
            vmem_limit_bytes=VMEM_LIMIT),
        name="proj0",
    )(x2, g, w_bf)


def _attn_kernel(lamp_ref, q_ref, k_ref, v_ref, za_ref, hn_ref, o_ref,
                 m_sc, l_sc, acc_sc, *, tq, tk):
    qi = pl.program_id(2)
    q = q_ref[...]
    q1 = q[:, :A_HEAD_DIM]
    q2 = q[:, A_HEAD_DIM:]
    m_sc[...] = jnp.full(m_sc.shape, -jnp.inf, jnp.float32)
    l_sc[...] = jnp.zeros(l_sc.shape, jnp.float32)
    acc_sc[...] = jnp.zeros(acc_sc.shape, jnp.float32)
    nt = (((1,), (1,)), ((), ()))

    def step(kstart, width, masked):
        k = k_ref[pl.ds(kstart, width), :]
        v = v_ref[pl.ds(kstart, width), :]
        s1 = lax.dot_general(q1, k[:, :A_HEAD_DIM], nt,
                             preferred_element_type=jnp.float32)
        s2 = lax.dot_general(q2, k[:, A_HEAD_DIM:], nt,
                             preferred_element_type=jnp.float32)
        s = jnp.concatenate([s1, s2], axis=0)
        if masked:
            rq = lax.broadcasted_iota(jnp.int32, (2 * tq, width), 0)
            rq = jnp.where(rq >= tq, rq - tq, rq) // CHUNK
            ck = lax.broadcasted_iota(jnp.int32, (2 * tq, width), 1) // CHUNK
            s = jnp.where(ck <= rq, s, -jnp.inf)
        m_prev = m_sc[...]
        m_new = jnp.maximum(m_prev, jnp.max(s, axis=1, keepdims=True))
        alpha = jnp.exp2(m_prev - m_new)
        p = jnp.exp2(s - pltpu.repeat(m_new, width // LANES, 1))
        l_sc[...] = alpha * l_sc[...] + jnp.sum(p, axis=1, keepdims=True)
        m_sc[...] = m_new
        pv = jnp.dot(p.astype(jnp.bfloat16), v,
                     preferred_element_type=jnp.float32)
        acc_sc[...] = acc_sc[...] * pltpu.repeat(alpha, A_VDIM // LANES, 1) + pv

    q_start = qi * tq
    n_big = q_start // tk

    def big_body(i, c):
        step(pl.multiple_of(i * tk, tk), tk, False)
        return c

    lax.fori_loop(0, n_big, big_body, 0)
    for r in range(tk // tq - 1):
        @pl.when(n_big * tk + r * tq < q_start)
        def _():
            step(pl.multiple_of(n_big * tk + r * tq, tq), tq, False)
    step(pl.multiple_of(q_start, tq), tq, True)

    lp = lamp_ref[...]
    d1 = jnp.sum(lp[0:1, :] * lp[1:2, :], axis=1, keepdims=True)
    d2 = jnp.sum(lp[2:3, :] * lp[3:4, :], axis=1, keepdims=True)
    lam = jnp.exp(d1) - jnp.exp(d2) + LAM_INIT_0

    inv_l = 1.0 / l_sc[...]
    acc = acc_sc[...] * pltpu.repeat(inv_l, A_VDIM // LANES, 1)
    o = acc[:tq, :] - lam * acc[tq:, :]
    ms = jnp.mean(o * o, axis=-1, keepdims=True)
    o = o * lax.rsqrt(ms + NORM_EPS) * hn_ref[...] * (1.0 - LAM_INIT_0)
    o_ref[...] = (o * za_ref[...].astype(jnp.float32)).astype(o_ref.dtype)


def _attention(proj3, lamp, hn, *, tq=256, tk=512):
    b, s, _ = proj3.shape
    nblk = A_WIDTH // A_VDIM
    return pl.pallas_call(
        functools.partial(_attn_kernel, tq=tq, tk=tk),
        grid=(b, A_HEADS, s // tq),
        in_specs=[
            pl.BlockSpec((4, A_HEAD_DIM), lambda bi, h, qi: (0, 0)),
            pl.BlockSpec((None, tq, A_VDIM), lambda bi, h, qi: (bi, qi, h)),
            pl.BlockSpec((None, s, A_VDIM), lambda bi, h, qi: (bi, 0, nblk + h)),
            pl.BlockSpec((None, s, A_VDIM), lambda bi, h, qi: (bi, 0, 2 * nblk + h)),
            pl.BlockSpec((None, tq, A_VDIM), lambda bi, h, qi: (bi, qi, 3 * nblk + h)),
            pl.BlockSpec((1, A_VDIM), lambda bi, h, qi: (0, h)),
        ],
        out_specs=pl.BlockSpec((None, tq, A_VDIM), lambda bi, h, qi: (bi, qi, h)),
        out_shape=jax.ShapeDtypeStruct((b, s, A_WIDTH), jnp.bfloat16),
        scratch_shapes=[
            pltpu.VMEM((2 * tq, LANES), jnp.float32),
            pltpu.VMEM((2 * tq, LANES), jnp.float32),
            pltpu.VMEM((2 * tq, A_VDIM), jnp.float32),
        ],
        compiler_params=pltpu.CompilerParams(
            dimension_semantics=("arbitrary", "arbitrary", "arbitrary"),
            vmem_limit_bytes=VMEM_LIMIT),
        name="diff_attn",
    )(lamp, proj3, proj3, proj3, proj3, hn)


def _sgu_kernel(gu_ref, gv_ref, zb_ref, lng_ref, lnb_ref, ws_ref, bst_ref,
                o_ref, *, rows):
    v = gv_ref[...].astype(jnp.float32)
    mu = jnp.mean(v, axis=-1, keepdims=True)
    vc = v - mu
    var = jnp.mean(vc * vc, axis=-1, keepdims=True)
    vln = (vc * lax.rsqrt(var + NORM_EPS) * lng_ref[...] + lnb_ref[...])
    vln = vln.astype(jnp.bfloat16)
    ci = lax.broadcasted_iota(jnp.int32, (B_SPAN, B_SPAN), 0) // CHUNK
    cj = lax.broadcasted_iota(jnp.int32, (B_SPAN, B_SPAN), 1) // CHUNK
    keep = ci >= cj
    for g in range(B_GROUPS):
        w = jnp.where(keep, ws_ref[g], 0.0).astype(jnp.bfloat16)
        bcol = bst_ref[:, g:g + 1]
        c0 = g * B_GROUP_DIM
        for sp in range(rows // B_SPAN):
            r0 = sp * B_SPAN
            mixed = jnp.dot(w, vln[r0:r0 + B_SPAN, c0:c0 + B_GROUP_DIM],
                            preferred_element_type=jnp.float32) + bcol
            u = gu_ref[r0:r0 + B_SPAN, c0:c0 + B_GROUP_DIM].astype(jnp.float32)
            z = zb_ref[r0:r0 + B_SPAN, c0:c0 + B_GROUP_DIM].astype(jnp.float32)
            o_ref[r0:r0 + B_SPAN, c0:c0 + B_GROUP_DIM] = (u * mixed * z).astype(o_ref.dtype)


def _sgu(proj, ln_g, ln_b, w_s, b_st, *, rows=256):
    t = proj.shape[0]
    base = 4 * A_WIDTH // B_WIDTH
    return pl.pallas_call(
        functools.partial(_sgu_kernel, rows=rows),
        grid=(t // rows,),
        in_specs=[
            pl.BlockSpec((rows, B_WIDTH), lambda i: (i, base)),
            pl.BlockSpec((rows, B_WIDTH), lambda i: (i, base + 1)),
            pl.BlockSpec((rows, B_WIDTH), lambda i: (i, base + 2)),
            pl.BlockSpec((1, B_WIDTH), lambda i: (0, 0)),
            pl.BlockSpec((1, B_WIDTH), lambda i: (0, 0)),
            pl.BlockSpec((B_GROUPS, B_SPAN, B_SPAN), lambda i: (0, 0, 0)),
            pl.BlockSpec((B_SPAN, B_GROUPS), lambda i: (0, 0)),
        ],
        out_specs=pl.BlockSpec((rows, B_WIDTH), lambda i: (i, 0)),
        out_shape=jax.ShapeDtypeStruct((t, B_WIDTH), jnp.bfloat16),
        compiler_params=pltpu.CompilerParams(
            dimension_semantics=("arbitrary",),
            vmem_limit_bytes=VMEM_LIMIT),
        name="sgu",
    )(proj, proj, proj, ln_g, ln_b, w_s, b_st)


def _out0_kernel(ya_ref, yb_ref, x_ref, wa_ref, wb_ref, g_ref, x1_ref, xn_ref):
    y = jnp.dot(ya_ref[...], wa_ref[...], preferred_element_type=jnp.float32)
    y = y + jnp.dot(yb_ref[...], wb_ref[...], preferred_element_type=jnp.float32)
    x1 = x_ref[...] + y
    x1_ref[...] = x1
    xn_ref[...] = _rms_rows(x1, g_ref[...]).astype(xn_ref.dtype)


def _out0(ya, yb, x2, w_bf, g, *, tm=256):
    t = x2.shape[0]
    const = pl.Buffered(1)
    return pl.pallas_call(
        _out0_kernel,
        grid=(t // tm,),
        in_specs=[
            pl.BlockSpec((tm, A_WIDTH), lambda i: (i, 0)),
            pl.BlockSpec((tm, B_WIDTH), lambda i: (i, 0)),
            pl.BlockSpec((tm, D_MODEL), lambda i: (i, 0)),
            pl.BlockSpec((A_WIDTH, D_MODEL), lambda i: (0, 0), pipeline_mode=const),
            pl.BlockSpec((B_WIDTH, D_MODEL), lambda i: (1, 0), pipeline_mode=const),
            pl.BlockSpec((1, D_MODEL), lambda i: (0, 0)),
        ],
        out_specs=[
            pl.BlockSpec((tm, D_MODEL), lambda i: (i, 0)),
            pl.BlockSpec((tm, D_MODEL), lambda i: (i, 0)),
        ],
        out_shape=[
            jax.ShapeDtypeStruct((t, D_MODEL), jnp.float32),
            jax.ShapeDtypeStruct((t, D_MODEL), jnp.bfloat16),
        ],
        compiler_params=pltpu.CompilerParams(
            dimension_semantics=("arbitrary",),
            vmem_limit_bytes=VMEM_LIMIT),
        name="out0",
    )(ya, yb, x2, w_bf, w_bf, g)


def _proj1_kernel(xn_ref, w_ref, o_ref):
    acc = jnp.dot(xn_ref[...], w_ref[...], preferred_element_type=jnp.float32)

    @pl.when(pl.program_id(0) == 0)
    def _():
        o_ref[...] = acc.astype(o_ref.dtype)

    @pl.when(pl.program_id(0) == 1)
    def _():
        o_ref[...] = _silu(acc).astype(o_ref.dtype)


def _proj1(xn, w_bf, *, tm=512):
    t = xn.shape[0]
    return pl.pallas_call(
        _proj1_kernel,
        grid=(2, t // tm),
        in_specs=[
            pl.BlockSpec((tm, D_MODEL), lambda j, i: (i, 0)),
            pl.BlockSpec((D_MODEL, C_WIDTH), lambda j, i: (0, j)),
        ],
        out_specs=pl.BlockSpec((tm, C_WIDTH), lambda j, i: (i, j)),
        out_shape=jax.ShapeDtypeStruct((t, 2 * C_WIDTH), jnp.bfloat16),
        compiler_params=pltpu.CompilerParams(
            dimension_semantics=("arbitrary", "arbitrary"),
            vmem_limit_bytes=VMEM_LIMIT),
        name="proj1",
    )(xn, w_bf)


def _rglru_kernel(xb_ref, zc_ref, cw_ref, cb_ref, wg_ref, ba_ref, bx_ref,
                  lam_ref, o_ref, pad_sc, xc_sc, xcb_sc, ga_sc, gx_sc, h_sc,
                  *, ts, rows):
    si = pl.program_id(1)
    halo = SUBLANES

    @pl.when(si == 0)
    def _():
        pad_sc[0:halo, :] = jnp.zeros((halo, C_WIDTH), jnp.float32)
        h_sc[...] = jnp.zeros(h_sc.shape, jnp.float32)

    @pl.when(si > 0)
    def _():
        pad_sc[0:halo, :] = pad_sc[ts:ts + halo, :]

    pad_sc[halo:halo + ts, :] = xb_ref[...].astype(jnp.float32)
    xc = cb_ref[...] + cw_ref[C_CONV - 1:C_CONV, :] * pad_sc[halo:halo + ts, :]
    for tap in range(C_CONV - 1):
        off = halo - (C_CONV - 1) + tap
        xc = xc + cw_ref[tap:tap + 1, :] * pad_sc[off:off + ts, :]
    xc_sc[...] = xc
    xcb_sc[...] = xc.astype(jnp.bfloat16)

    for j in range(N_CT):
        ws = GATE_WS[j]
        g = jnp.dot(xcb_sc[:, ws:ws + GATE_WIN], wg_ref[j],
                    preferred_element_type=jnp.float32)
        ga_sc[:, j * LANES:(j + 1) * LANES] = g[:, :LANES]
        gx_sc[:, j * LANES:(j + 1) * LANES] = g[:, LANES:]

    lam = lam_ref[...]
    sp = jnp.maximum(-lam, 0.0) + jnp.log1p(jnp.exp(-jnp.abs(lam)))
    neg_c_sp = -C_GATE_C * sp
    ba = ba_ref[...]
    bx = bx_ref[...]
    sub = lax.broadcasted_iota(jnp.int32, (SUBLANES, C_WIDTH), 0)

    def body(it, h_prev):
        r0 = pl.multiple_of(it * rows, rows)
        outs = []
        for gi in range(rows // SUBLANES):
            rs = pl.ds(r0 + gi * SUBLANES, SUBLANES)
            x = xc_sc[rs, :]
            r = 1.0 / (1.0 + jnp.exp(-(ga_sc[rs, :] + ba)))
            i = 1.0 / (1.0 + jnp.exp(-(gx_sc[rs, :] + bx)))
            log_a = neg_c_sp * r
            a = jnp.exp(log_a)
            th = jnp.tanh(log_a)
            beta = jnp.sqrt(-2.0 * th / (1.0 - th))
            b = beta * (i * x)
            for d in (1, 2, 4):
                keep = sub >= d
                a_sh = pltpu.roll(a, d, 0)
                b_sh = pltpu.roll(b, d, 0)
                b = jnp.where(keep, a * b_sh + b, b)
                a = jnp.where(keep, a * a_sh, a)
            h = a * h_prev + b
            h_prev = jnp.broadcast_to(h[SUBLANES - 1:SUBLANES, :], (SUBLANES, C_WIDTH))
            outs.append(h)
        hh = jnp.concatenate(outs, axis=0)
        z = zc_ref[pl.ds(r0, rows), :].astype(jnp.float32)
        o_ref[pl.ds(r0, rows), :] = (hh * z).astype(o_ref.dtype)
        return h_prev

    h_sc[...] = lax.fori_loop(0, ts // rows, body, h_sc[...])


def _rglru(xbz3, cw, cb, wg, ba, bx, lam, *, ts=256, rows=16):
    b, s, _ = xbz3.shape
    const = pl.Buffered(1)
    vec = lambda: pl.BlockSpec((1, C_WIDTH), lambda bi, si: (0, 0))
    return pl.pallas_call(
        functools.partial(_rglru_kernel, ts=ts, rows=rows),
        grid=(b, s // ts),
        in_specs=[
            pl.BlockSpec((None, ts, C_WIDTH), lambda bi, si: (bi, si, 0)),
            pl.BlockSpec((None, ts, C_WIDTH), lambda bi, si: (bi, si, 1)),
            pl.BlockSpec((C_CONV, C_WIDTH), lambda bi, si: (0, 0)),
            vec(),
            pl.BlockSpec((N_CT, GATE_WIN, 2 * LANES), lambda bi, si: (0, 0, 0),
                         pipeline_mode=const),
            vec(), vec(), vec(),
        ],
        out_specs=pl.BlockSpec((None, ts, C_WIDTH), lambda bi, si: (bi, si, 0)),
        out_shape=jax.ShapeDtypeStruct((b, s, C_WIDTH), jnp.bfloat16),
        scratch_shapes=[
            pltpu.VMEM((ts + SUBLANES, C_WIDTH), jnp.float32),
            pltpu.VMEM((ts, C_WIDTH), jnp.float32),
            pltpu.VMEM((ts, C_WIDTH), jnp.bfloat16),
            pltpu.VMEM((ts, C_WIDTH), jnp.float32),
            pltpu.VMEM((ts, C_WIDTH), jnp.float32),
            pltpu.VMEM((SUBLANES, C_WIDTH), jnp.float32),
        ],
        compiler_params=pltpu.CompilerParams(
            dimension_semantics=("arbitrary", "arbitrary"),
            vmem_limit_bytes=VMEM_LIMIT),
        name="rglru",
    )(xbz3, xbz3, cw, cb, wg, ba, bx, lam)


def _out1_kernel(hz_ref, x_ref, w_ref, g_ref, o_ref):
    y = jnp.dot(hz_ref[...], w_ref[...], preferred_element_type=jnp.float32)
    o_ref[...] = _rms_rows(x_ref[...] + y, g_ref[...])


def _out1(hz, x1, w_bf, g, *, tm=256):
    t = x1.shape[0]
    return pl.pallas_call(
        _out1_kernel,
        grid=(t // tm,),
        in_specs=[
            pl.BlockSpec((tm, C_WIDTH), lambda i: (i, 0)),
            pl.BlockSpec((tm, D_MODEL), lambda i: (i, 0)),
            pl.BlockSpec((C_WIDTH, D_MODEL), lambda i: (0, 0),
                         pipeline_mode=pl.Buffered(1)),
            pl.BlockSpec((1, D_MODEL), lambda i: (0, 0)),
        ],
        out_specs=pl.BlockSpec((tm, D_MODEL), lambda i: (i, 0)),
        out_shape=jax.ShapeDtypeStruct((t, D_MODEL), jnp.float32),
        compiler_params=pltpu.CompilerParams(
            dimension_semantics=("arbitrary",),
            vmem_limit_bytes=VMEM_LIMIT),
        name="out1",
    )(hz, x1, w_bf, g)


def _gate_weights(w_a, w_x):
    def dense(w):
        eye = jnp.eye(C_HEADS, dtype=w.dtype)
        return jnp.einsum('hij,hg->higj', w, eye).reshape(C_WIDTH, C_WIDTH)
    da, dx = dense(w_a), dense(w_x)
    slabs = []
    for j in range(N_CT):
        ws = GATE_WS[j]
        cols = slice(j * LANES, (j + 1) * LANES)
        slabs.append(jnp.concatenate(
            [da[ws:ws + GATE_WIN, cols], dx[ws:ws + GATE_WIN, cols]], axis=1))
    return jnp.stack(slabs).astype(jnp.bfloat16)


def kernel(x, ab_norm, ab_w_in, ab_lambda, ab_head_norm, ab_sgu_ln_g, ab_sgu_ln_b, ab_sgu_w, ab_sgu_b, ab_w_out, c_norm, c_w_in, c_conv_w, c_conv_b, c_gate_a_w, c_gate_a_b, c_gate_x_w, c_gate_x_b, c_lambda, c_w_out, final_norm):
    bsz, seq, d = x.shape
    t = bsz * seq
    bf = jnp.bfloat16
    row = lambda v: v.reshape(1, -1)
    x2 = x.reshape(t, d)

    proj = _proj0(x2, row(ab_norm[0]), ab_w_in[0].astype(bf))
    ya = _attention(proj.reshape(bsz, seq, AB_IN), ab_lambda[0], row(ab_head_norm[0]))
    yb = _sgu(proj, row(ab_sgu_ln_g[0]), row(ab_sgu_ln_b[0]), ab_sgu_w[0], ab_sgu_b[0].T)
    x1, xn1 = _out0(ya.reshape(t, A_WIDTH), yb, x2, ab_w_out[0].astype(bf), row(c_norm[0]))

    xbz = _proj1(xn1, c_w_in[0].astype(bf))
    wg = _gate_weights(c_gate_a_w[0], c_gate_x_w[0])
    hz = _rglru(xbz.reshape(bsz, seq, 2 * C_WIDTH), c_conv_w[0], row(c_conv_b[0]), wg,
                row(c_gate_a_b[0]), row(c_gate_x_b[0]), row(c_lambda[0]))
    out = _out1(hz.reshape(t, C_WIDTH), x1, c_w_out[0].astype(bf), row(final_norm))
    return out.reshape(bsz, seq, d)
```

```python
import functools
import math

import jax
import jax.numpy as jnp
import numpy as np
from jax import lax
from jax.experimental import pallas as pl
from jax.experimental.pallas import tpu as pltpu

D_MODEL = 2048
CHUNK = 64
CHUNK_SHIFT = 6
NORM_EPS = 1e-6
A_HEADS = 8
A_HEAD_DIM = 128
A_VDIM = 2 * A_HEAD_DIM
A_WIDTH = A_HEADS * A_VDIM
B_GROUPS = 8
B_WIDTH = D_MODEL
B_GROUP_DIM = B_WIDTH // B_GROUPS
B_SPAN = 128
C_WIDTH = 2688
C_HEADS = 16
C_BLOCK = C_WIDTH // C_HEADS
C_CONV = 4
C_GATE_C = 8.0
AB_IN = 4 * A_WIDTH + 3 * B_WIDTH
LAM_INIT_0 = 0.8 - 0.6 * math.exp(-0.3 * 0)

LANES = 128
SUBLANES = 8
LOG2E = 1.4426950408889634
VMEM_LIMIT = 56 * 1024 * 1024

GATE_WIN = 512
GATE_PAD = 256
N_CT = C_WIDTH // LANES


def _gate_window_start(j):
    h0 = (LANES * j) // C_BLOCK
    ws = (C_BLOCK * h0) // LANES * LANES
    return min(ws, C_WIDTH - GATE_WIN)


GATE_WS = tuple(_gate_window_start(j) for j in range(N_CT))


def _silu(z):
    return z * (1.0 / (1.0 + jnp.exp(-z)))


def _gelu_tanh(x):
    c = math.sqrt(2.0 / math.pi)
    return 0.5 * x * (1.0 + jnp.tanh(c * (x + 0.044715 * (x * x * x))))


def _rms_rows(xf, g):
    ms = jnp.mean(xf * xf, axis=-1, keepdims=True)
    return xf * lax.rsqrt(ms + NORM_EPS) * g


def _proj0_kernel(x_ref, g_ref, w_ref, o_ref, xn_sc, *, tiles_per_seg):
    j = pl.program_id(1)

    @pl.when(j == 0)
    def _():
        xn_sc[...] = _rms_rows(x_ref[...], g_ref[...]).astype(jnp.bfloat16)

    seg = j // tiles_per_seg

    def proj():
        return jnp.dot(xn_sc[...], w_ref[...], preferred_element_type=jnp.float32)

    @pl.when(seg <= 2)
    def _():
        scale = jnp.where(seg == 0, A_HEAD_DIM ** -0.5 * LOG2E, 1.0).astype(jnp.float32)
        o_ref[...] = (proj() * scale).astype(o_ref.dtype)

    @pl.when((seg == 3) | (seg == 6))
    def _():
        o_ref[...] = _silu(proj()).astype(o_ref.dtype)

    @pl.when((seg == 4) | (seg == 5))
    def _():
        o_ref[...] = _gelu_tanh(proj()).astype(o_ref.dtype)


def _proj0(x2, g, w_bf, *, tm=1024, tn=1024):
    t = x2.shape[0]
    n = w_bf.shape[1]
    return pl.pallas_call(
        functools.partial(_proj0_kernel, tiles_per_seg=A_WIDTH // tn),
        grid=(t // tm, n // tn),
        in_specs=[
            pl.BlockSpec((tm, D_MODEL), lambda i, j: (i, 0)),
            pl.BlockSpec((1, D_MODEL), lambda i, j: (0, 0)),
            pl.BlockSpec((D_MODEL, tn), lambda i, j: (0, j)),
        ],
        out_specs=pl.BlockSpec((tm, tn), lambda i, j: (i, j)),
        out_shape=jax.ShapeDtypeStruct((t, n), jnp.bfloat16),
        scratch_shapes=[pltpu.VMEM((tm, D_MODEL), jnp.bfloat16)],
        compiler_params=pltpu.CompilerParams(
            dimension_semantics=("arbitrary", "arbitrary"),
            vmem_limit_bytes=VMEM_LIMIT),
        name="proj0",
    )(x2, g, w_bf)


def _attn_kernel(lamp_ref, q_ref, k_ref, v_ref, za_ref, hn_ref, o_ref,
                 m_sc, l_sc, acc_sc, sa_sc, sb_sc, *, t):
    qi = pl.program_id(2)
    m_sc[...] = jnp.full(m_sc.shape, -jnp.inf, jnp.float32)
    l_sc[...] = jnp.zeros(l_sc.shape, jnp.float32)
    acc_sc[...] = jnp.zeros(acc_sc.shape, jnp.float32)
    nt = (((1,), (1,)), ((), ()))

    def scores(j, s_ref):
        ks = pl.multiple_of(j * t, t)
        s_ref[0:t, :] = lax.dot_general(
            q_ref[:, :A_HEAD_DIM], k_ref[pl.ds(ks, t), :A_HEAD_DIM], nt,
            preferred_element_type=jnp.float32)
        s_ref[t:2 * t, :] = lax.dot_general(
            q_ref[:, A_HEAD_DIM:], k_ref[pl.ds(ks, t), A_HEAD_DIM:], nt,
            preferred_element_type=jnp.float32)

    def softmax_pv(j, s_ref, masked):
        ks = pl.multiple_of(j * t, t)
        s = s_ref[...]
        if masked:
            rq = lax.broadcasted_iota(jnp.int32, (2 * t, 1), 0)
            rq = lax.shift_right_logical(jnp.where(rq >= t, rq - t, rq), CHUNK_SHIFT)
            ck = lax.shift_right_logical(
                lax.broadcasted_iota(jnp.int32, (1, t), 1), CHUNK_SHIFT)
            s = jnp.where(ck <= rq, s, -jnp.inf)
        m_prev = m_sc[...]
        m_new = jnp.maximum(m_prev, jnp.max(s, axis=1, keepdims=True))
        alpha = jnp.exp2(m_prev - m_new)
        p = jnp.exp2(s - pltpu.repeat(m_new, t // LANES, 1))
        l_sc[...] = alpha * l_sc[...] + jnp.sum(p, axis=1, keepdims=True)
        m_sc[...] = m_new
        pv = jnp.dot(p.astype(jnp.bfloat16), v_ref[pl.ds(ks, t), :],
                     preferred_element_type=jnp.float32)
        acc_sc[...] = acc_sc[...] * pltpu.repeat(alpha, A_VDIM // LANES, 1) + pv

    scores(0, sa_sc)

    def pair(p, c):
        j = 2 * p
        scores(j + 1, sb_sc)
        softmax_pv(j, sa_sc, False)
        scores(j + 2, sa_sc)
        softmax_pv(j + 1, sb_sc, False)
        return c

    lax.fori_loop(0, qi // 2, pair, 0)

    @pl.when(qi % 2 == 1)
    def _():
        scores(qi, sb_sc)
        softmax_pv(qi - 1, sa_sc, False)
        softmax_pv(qi, sb_sc, True)

    @pl.when(qi % 2 == 0)
    def _():
        softmax_pv(qi, sa_sc, True)

    lp = lamp_ref[...]
    d1 = jnp.sum(lp[0:1, :] * lp[1:2, :], axis=1, keepdims=True)
    d2 = jnp.sum(lp[2:3, :] * lp[3:4, :], axis=1, keepdims=True)
    lam = jnp.exp(d1) - jnp.exp(d2) + LAM_INIT_0

    inv_l = 1.0 / l_sc[...]
    acc = acc_sc[...] * pltpu.repeat(inv_l, A_VDIM // LANES, 1)
    o = acc[:t, :] - lam * acc[t:, :]
    ms = jnp.mean(o * o, axis=-1, keepdims=True)
    o = o * lax.rsqrt(ms + NORM_EPS) * hn_ref[...] * (1.0 - LAM_INIT_0)
    o_ref[...] = (o * za_ref[...].astype(jnp.float32)).astype(o_ref.dtype)


def _attention(proj3, lamp, hn, *, t=512):
    b, s, _ = proj3.shape
    nblk = A_WIDTH // A_VDIM
    return pl.pallas_call(
        functools.partial(_attn_kernel, t=t),
        grid=(b, A_HEADS, s // t),
        in_specs=[
            pl.BlockSpec((4, A_HEAD_DIM), lambda bi, h, qi: (0, 0)),
            pl.BlockSpec((None, t, A_VDIM), lambda bi, h, qi: (bi, qi, h)),
            pl.BlockSpec((None, s, A_VDIM), lambda bi, h, qi: (bi, 0, nblk + h)),
            pl.BlockSpec((None, s, A_VDIM), lambda bi, h, qi: (bi, 0, 2 * nblk + h)),
            pl.BlockSpec((None, t, A_VDIM), lambda bi, h, qi: (bi, qi, 3 * nblk + h)),
            pl.BlockSpec((1, A_VDIM), lambda bi, h, qi: (0, h)),
        ],
        out_specs=pl.BlockSpec((None, t, A_VDIM), lambda bi, h, qi: (bi, qi, h)),
        out_shape=jax.ShapeDtypeStruct((b, s, A_WIDTH), jnp.bfloat16),
        scratch_shapes=[
            pltpu.VMEM((2 * t, LANES), jnp.float32),
            pltpu.VMEM((2 * t, LANES), jnp.float32),
            pltpu.VMEM((2 * t, A_VDIM), jnp.float32),
            pltpu.VMEM((2 * t, t), jnp.float32),
            pltpu.VMEM((2 * t, t), jnp.float32),
        ],
        compiler_params=pltpu.CompilerParams(
            dimension_semantics=("arbitrary", "arbitrary", "arbitrary"),
            vmem_limit_bytes=VMEM_LIMIT),
        name="diff_attn",
    )(lamp, proj3, proj3, proj3, proj3, hn)


def _sgu_kernel(gu_ref, gv_ref, zb_ref, lng_ref, lnb_ref, ws_ref, bst_ref,
                o_ref, *, rows):
    v = gv_ref[...].astype(jnp.float32)
    mu = jnp.mean(v, axis=-1, keepdims=True)
    vc = v - mu
    var = jnp.mean(vc * vc, axis=-1, keepdims=True)
    vln = (vc * lax.rsqrt(var + NORM_EPS) * lng_ref[...] + lnb_ref[...])
    vln = vln.astype(jnp.bfloat16)
    ci = lax.broadcasted_iota(jnp.int32, (B_SPAN, B_SPAN), 0) // CHUNK
    cj = lax.broadcasted_iota(jnp.int32, (B_SPAN, B_SPAN), 1) // CHUNK
    keep = ci >= cj
    for g in range(B_GROUPS):
        w = jnp.where(keep, ws_ref[g], 0.0).astype(jnp.bfloat16)
        bcol = bst_ref[:, g:g + 1]
        c0 = g * B_GROUP_DIM
        for sp in range(rows // B_SPAN):
            r0 = sp * B_SPAN
            mixed = jnp.dot(w, vln[r0:r0 + B_SPAN, c0:c0 + B_GROUP_DIM],
                            preferred_element_type=jnp.float32) + bcol
            u = gu_ref[r0:r0 + B_SPAN, c0:c0 + B_GROUP_DIM].astype(jnp.float32)
            z = zb_ref[r0:r0 + B_SPAN, c0:c0 + B_GROUP_DIM].astype(jnp.float32)
            o_ref[r0:r0 + B_SPAN, c0:c0 + B_GROUP_DIM] = (u * mixed * z).astype(o_ref.dtype)


def _sgu(proj, ln_g, ln_b, w_s, b_st, *, rows=256):
    t = proj.shape[0]
    base = 4 * A_WIDTH // B_WIDTH
    return pl.pallas_call(
        functools.partial(_sgu_kernel, rows=rows),
        grid=(t // rows,),
        in_specs=[
            pl.BlockSpec((rows, B_WIDTH), lambda i: (i, base)),
            pl.BlockSpec((rows, B_WIDTH), lambda i: (i, base + 1)),
            pl.BlockSpec((rows, B_WIDTH), lambda i: (i, base + 2)),
            pl.BlockSpec((1, B_WIDTH), lambda i: (0, 0)),
            pl.BlockSpec((1, B_WIDTH), lambda i: (0, 0)),
            pl.BlockSpec((B_GROUPS, B_SPAN, B_SPAN), lambda i: (0, 0, 0)),
            pl.BlockSpec((B_SPAN, B_GROUPS), lambda i: (0, 0)),
        ],
        out_specs=pl.BlockSpec((rows, B_WIDTH), lambda i: (i, 0)),
        out_shape=jax.ShapeDtypeStruct((t, B_WIDTH), jnp.bfloat16),
        compiler_params=pltpu.CompilerParams(
            dimension_semantics=("arbitrary",),
            vmem_limit_bytes=VMEM_LIMIT),
        name="sgu",
    )(proj, proj, proj, ln_g, ln_b, w_s, b_st)


def _out0_kernel(ya_ref, yb_ref, x_ref, wa_ref, wb_ref, g_ref, x1_ref, xn_ref):
    y = jnp.dot(ya_ref[...], wa_ref[...], preferred_element_type=jnp.float32)
    y = y + jnp.dot(yb_ref[...], wb_ref[...], preferred_element_type=jnp.float32)
    x1 = x_ref[...] + y
    x1_ref[...] = x1
    xn_ref[...] = _rms_rows(x1, g_ref[...]).astype(xn_ref.dtype)


def _out0(ya, yb, x2, w_bf, g, *, tm=256):
    t = x2.shape[0]
    const = pl.Buffered(1)
    return pl.pallas_call(
        _out0_kernel,
        grid=(t // tm,),
        in_specs=[
            pl.BlockSpec((tm, A_WIDTH), lambda i: (i, 0)),
            pl.BlockSpec((tm, B_WIDTH), lambda i: (i, 0)),
            pl.BlockSpec((tm, D_MODEL), lambda i: (i, 0)),
            pl.BlockSpec((A_WIDTH, D_MODEL), lambda i: (0, 0), pipeline_mode=const),
            pl.BlockSpec((B_WIDTH, D_MODEL), lambda i: (1, 0), pipeline_mode=const),
            pl.BlockSpec((1, D_MODEL), lambda i: (0, 0)),
        ],
        out_specs=[
            pl.BlockSpec((tm, D_MODEL), lambda i: (i, 0)),
            pl.BlockSpec((tm, D_MODEL), lambda i: (i, 0)),
        ],
        out_shape=[
            jax.ShapeDtypeStruct((t, D_MODEL), jnp.float32),
            jax.ShapeDtypeStruct((t, D_MODEL), jnp.bfloat16),
        ],
        compiler_params=pltpu.CompilerParams(
            dimension_semantics=("arbitrary",),
            vmem_limit_bytes=VMEM_LIMIT),
        name="out0",
    )(ya, yb, x2, w_bf, w_bf, g)


def _proj1_kernel(xn_ref, w_ref, o_ref):
    def proj():
        return jnp.dot(xn_ref[...], w_ref[...], preferred_element_type=jnp.float32)

    @pl.when(pl.program_id(0) == 0)
    def _():
        o_ref[...] = proj().astype(o_ref.dtype)

    @pl.when(pl.program_id(0) == 1)
    def _():
        o_ref[...] = _silu(proj()).astype(o_ref.dtype)


def _proj1(xn, w_bf, *, tm=512):
    t = xn.shape[0]
    return pl.pallas_call(
        _proj1_kernel,
        grid=(2, t // tm),
        in_specs=[
            pl.BlockSpec((tm, D_MODEL), lambda j, i: (i, 0)),
            pl.BlockSpec((D_MODEL, C_WIDTH), lambda j, i: (0, j)),
        ],
        out_specs=pl.BlockSpec((tm, C_WIDTH), lambda j, i: (i, j)),
        out_shape=jax.ShapeDtypeStruct((t, 2 * C_WIDTH), jnp.bfloat16),
        compiler_params=pltpu.CompilerParams(
            dimension_semantics=("arbitrary", "arbitrary"),
            vmem_limit_bytes=VMEM_LIMIT),
        name="proj1",
    )(xn, w_bf)


def _build_gate_slabs(wp_ref, wg_sc, tmp_sc):
    row = lax.broadcasted_iota(jnp.int32, (GATE_PAD, LANES), 0)
    col = lax.broadcasted_iota(jnp.int32, (GATE_PAD, LANES), 1)
    for j in range(N_CT):
        tmp_sc[...] = jnp.zeros(tmp_sc.shape, jnp.float32)
        h_lo = (LANES * j) // C_BLOCK
        h_hi = (LANES * j + LANES - 1) // C_BLOCK
        for gate in range(2):
            for h in range(h_lo, h_hi + 1):
                sel = jnp.where(row == col + (LANES * j - C_BLOCK * h), 1.0, 0.0)
                part = jnp.dot(wp_ref[gate, h], sel.astype(jnp.bfloat16),
                               preferred_element_type=jnp.float32)
                r0 = C_BLOCK * h - GATE_WS[j]
                cs = slice(gate * LANES, (gate + 1) * LANES)
                tmp_sc[r0:r0 + C_BLOCK, cs] = tmp_sc[r0:r0 + C_BLOCK, cs] + part[:C_BLOCK, :]
        wg_sc[j] = tmp_sc[...].astype(jnp.bfloat16)


def _rglru_kernel(xb_ref, zc_ref, cw_ref, cb_ref, wp_ref, ba_ref, bx_ref,
                  lam_ref, o_ref, pad_sc, xc_sc, xcb_sc, ga_sc, gx_sc, h_sc,
                  wg_ref, tmp_sc, *, ts, rows):
    si = pl.program_id(1)
    halo = SUBLANES

    @pl.when((pl.program_id(0) == 0) & (si == 0))
    def _():
        _build_gate_slabs(wp_ref, wg_ref, tmp_sc)

    @pl.when(si == 0)
    def _():
        pad_sc[0:halo, :] = jnp.zeros((halo, C_WIDTH), jnp.float32)
        h_sc[...] = jnp.zeros(h_sc.shape, jnp.float32)

    @pl.when(si > 0)
    def _():
        pad_sc[0:halo, :] = pad_sc[ts:ts + halo, :]

    pad_sc[halo:halo + ts, :] = xb_ref[...].astype(jnp.float32)
    xc = cb_ref[...] + cw_ref[C_CONV - 1:C_CONV, :] * pad_sc[halo:halo + ts, :]
    for tap in range(C_CONV - 1):
        off = halo - (C_CONV - 1) + tap
        xc = xc + cw_ref[tap:tap + 1, :] * pad_sc[off:off + ts, :]
    xc_sc[...] = xc
    xcb_sc[...] = xc.astype(jnp.bfloat16)

    for j in range(N_CT):
        ws = GATE_WS[j]
        g = jnp.dot(xcb_sc[:, ws:ws + GATE_WIN], wg_ref[j],
                    preferred_element_type=jnp.float32)
        ga_sc[:, j * LANES:(j + 1) * LANES] = g[:, :LANES]
        gx_sc[:, j * LANES:(j + 1) * LANES] = g[:, LANES:]

    lam = lam_ref[...]
    sp = jnp.maximum(-lam, 0.0) + jnp.log1p(jnp.exp(-jnp.abs(lam)))
    neg_c_sp = -C_GATE_C * sp
    ba = ba_ref[...]
    bx = bx_ref[...]
    sub = lax.broadcasted_iota(jnp.int32, (SUBLANES, C_WIDTH), 0)

    def body(it, h_prev):
        r0 = pl.multiple_of(it * rows, rows)
        outs = []
        for gi in range(rows // SUBLANES):
            rs = pl.ds(r0 + gi * SUBLANES, SUBLANES)
            x = xc_sc[rs, :]
            r = 1.0 / (1.0 + jnp.exp(-(ga_sc[rs, :] + ba)))
            i = 1.0 / (1.0 + jnp.exp(-(gx_sc[rs, :] + bx)))
            log_a = neg_c_sp * r
            a = jnp.exp(log_a)
            th = jnp.tanh(log_a)
            beta = jnp.sqrt(-2.0 * th / (1.0 - th))
            b = beta * (i * x)
            for d in (1, 2, 4):
                keep = sub >= d
                a_sh = pltpu.roll(a, d, 0)
                b_sh = pltpu.roll(b, d, 0)
                b = jnp.where(keep, a * b_sh + b, b)
                a = jnp.where(keep, a * a_sh, a)
            h = a * h_prev + b
            h_prev = jnp.broadcast_to(h[SUBLANES - 1:SUBLANES, :], (SUBLANES, C_WIDTH))
            outs.append(h)
        hh = jnp.concatenate(outs, axis=0)
        z = zc_ref[pl.ds(r0, rows), :].astype(jnp.float32)
        o_ref[pl.ds(r0, rows), :] = (hh * z).astype(o_ref.dtype)
        return h_prev

    h_sc[...] = lax.fori_loop(0, ts // rows, body, h_sc[...])


def _rglru(xbz3, cw, cb, wp, ba, bx, lam, *, ts=256, rows=16):
    b, s, _ = xbz3.shape
    const = pl.Buffered(1)
    vec = lambda: pl.BlockSpec((1, C_WIDTH), lambda bi, si: (0, 0))
    return pl.pallas_call(
        functools.partial(_rglru_kernel, ts=ts, rows=rows),
        grid=(b, s // ts),
        in_specs=[
            pl.BlockSpec((None, ts, C_WIDTH), lambda bi, si: (bi, si, 0)),
            pl.BlockSpec((None, ts, C_WIDTH), lambda bi, si: (bi, si, 1)),
            pl.BlockSpec((C_CONV, C_WIDTH), lambda bi, si: (0, 0)),
            vec(),
            pl.BlockSpec((2, C_HEADS, GATE_PAD, GATE_PAD), lambda bi, si: (0, 0, 0, 0),
                         pipeline_mode=const),
            vec(), vec(), vec(),
        ],
        out_specs=pl.BlockSpec((None, ts, C_WIDTH), lambda bi, si: (bi, si, 0)),
        out_shape=jax.ShapeDtypeStruct((b, s, C_WIDTH), jnp.bfloat16),
        scratch_shapes=[
            pltpu.VMEM((ts + SUBLANES, C_WIDTH), jnp.float32),
            pltpu.VMEM((ts, C_WIDTH), jnp.float32),
            pltpu.VMEM((ts, C_WIDTH), jnp.bfloat16),
            pltpu.VMEM((ts, C_WIDTH), jnp.float32),
            pltpu.VMEM((ts, C_WIDTH), jnp.float32),
            pltpu.VMEM((SUBLANES, C_WIDTH), jnp.float32),
            pltpu.VMEM((N_CT, GATE_WIN, 2 * LANES), jnp.bfloat16),
            pltpu.VMEM((GATE_WIN, 2 * LANES), jnp.float32),
        ],
        compiler_params=pltpu.CompilerParams(
            dimension_semantics=("arbitrary", "arbitrary"),
            vmem_limit_bytes=VMEM_LIMIT),
        name="rglru",
    )(xbz3, xbz3, cw, cb, wp, ba, bx, lam)


def _out1_kernel(hz_ref, x_ref, w_ref, g_ref, o_ref):
    y = jnp.dot(hz_ref[...], w_ref[...], preferred_element_type=jnp.float32)
    o_ref[...] = _rms_rows(x_ref[...] + y, g_ref[...])


def _out1(hz, x1, w_bf, g, *, tm=256):
    t = x1.shape[0]
    return pl.pallas_call(
        _out1_kernel,
        grid=(t // tm,),
        in_specs=[
            pl.BlockSpec((tm, C_WIDTH), lambda i: (i, 0)),
            pl.BlockSpec((tm, D_MODEL), lambda i: (i, 0)),
            pl.BlockSpec((C_WIDTH, D_MODEL), lambda i: (0, 0),
                         pipeline_mode=pl.Buffered(1)),
            pl.BlockSpec((1, D_MODEL), lambda i: (0, 0)),
        ],
        out_specs=pl.BlockSpec((tm, D_MODEL), lambda i: (i, 0)),
        out_shape=jax.ShapeDtypeStruct((t, D_MODEL), jnp.float32),
        compiler_params=pltpu.CompilerParams(
            dimension_semantics=("arbitrary",),
            vmem_limit_bytes=VMEM_LIMIT),
        name="out1",
    )(hz, x1, w_bf, g)


def kernel(x, ab_norm, ab_w_in, ab_lambda, ab_head_norm, ab_sgu_ln_g, ab_sgu_ln_b, ab_sgu_w, ab_sgu_b, ab_w_out, c_norm, c_w_in, c_conv_w, c_conv_b, c_gate_a_w, c_gate_a_b, c_gate_x_w, c_gate_x_b, c_lambda, c_w_out, final_norm):
    bsz, seq, d = x.shape
    t = bsz * seq
    bf = jnp.bfloat16
    row = lambda v: v.reshape(1, -1)
    x2 = x.reshape(t, d)

    proj = _proj0(x2, row(ab_norm[0]), ab_w_in[0].astype(bf))
    ya = _attention(proj.reshape(bsz, seq, AB_IN), ab_lambda[0], row(ab_head_norm[0]))
    yb = _sgu(proj, row(ab_sgu_ln_g[0]), row(ab_sgu_ln_b[0]), ab_sgu_w[0], ab_sgu_b[0].T)
    x1, xn1 = _out0(ya.reshape(t, A_WIDTH), yb, x2, ab_w_out[0].astype(bf), row(c_norm[0]))

    xbz = _proj1(xn1, c_w_in[0].astype(bf))
    padw = GATE_PAD - C_BLOCK
    wp = jnp.pad(jnp.stack([c_gate_a_w[0], c_gate_x_w[0]]).astype(bf),
                 ((0, 0), (0, 0), (0, padw), (0, padw)))
    hz = _rglru(xbz.reshape(bsz, seq, 2 * C_WIDTH), c_conv_w[0], row(c_conv_b[0]), wp,
                row(c_gate_a_b[0]), row(c_gate_x_b[0]), row(c_lambda[0]))
    out = _out1(hz.reshape(t, C_WIDTH), x1, c_w_out[0].astype(bf), row(final_norm))
    return out.reshape(bsz, seq, d)
```

```python
import functools
import math

import jax
import jax.numpy as jnp
import numpy as np
from jax import lax
from jax.experimental import pallas as pl
from jax.experimental.pallas import tpu as pltpu

D_MODEL = 2048
CHUNK = 64
CHUNK_SHIFT = 6
NORM_EPS = 1e-6
A_HEADS = 8
A_HEAD_DIM = 128
A_VDIM = 2 * A_HEAD_DIM
A_WIDTH = A_HEADS * A_VDIM
B_GROUPS = 8
B_WIDTH = D_MODEL
B_GROUP_DIM = B_WIDTH // B_GROUPS
B_SPAN = 128
C_WIDTH = 2688
C_HEADS = 16
C_BLOCK = C_WIDTH // C_HEADS
C_CONV = 4
C_GATE_C = 8.0
AB_IN = 4 * A_WIDTH + 3 * B_WIDTH
LAM_INIT_0 = 0.8 - 0.6 * math.exp(-0.3 * 0)

LANES = 128
SUBLANES = 8
LOG2E = 1.4426950408889634
VMEM_LIMIT = 56 * 1024 * 1024

GATE_WIN = 512
GATE_PAD = 256
N_CT = C_WIDTH // LANES


def _gate_window_start(j):
    h0 = (LANES * j) // C_BLOCK
    ws = (C_BLOCK * h0) // LANES * LANES
    return min(ws, C_WIDTH - GATE_WIN)


GATE_WS = tuple(_gate_window_start(j) for j in range(N_CT))


def _silu(z):
    return z * (1.0 / (1.0 + jnp.exp(-z)))


def _gelu_tanh(x):
    c = math.sqrt(2.0 / math.pi)
    return 0.5 * x * (1.0 + jnp.tanh(c * (x + 0.044715 * (x * x * x))))


def _rms_rows(xf, g):
    ms = jnp.mean(xf * xf, axis=-1, keepdims=True)
    return xf * lax.rsqrt(ms + NORM_EPS) * g


def _proj0_kernel(x_ref, g_ref, w_ref, o_ref, xn_sc, *, tiles_per_seg):
    j = pl.program_id(1)

    @pl.when(j == 0)
    def _():
        xn_sc[...] = _rms_rows(x_ref[...], g_ref[...]).astype(jnp.bfloat16)

    seg = j // tiles_per_seg

    def proj():
        return jnp.dot(xn_sc[...], w_ref[...], preferred_element_type=jnp.float32)

    @pl.when(seg <= 2)
    def _():
        scale = jnp.where(seg == 0, A_HEAD_DIM ** -0.5 * LOG2E, 1.0).astype(jnp.float32)
        o_ref[...] = (proj() * scale).astype(o_ref.dtype)

    @pl.when((seg == 3) | (seg == 6))
    def _():
        o_ref[...] = _silu(proj()).astype(o_ref.dtype)

    @pl.when((seg == 4) | (seg == 5))
    def _():
        o_ref[...] = _gelu_tanh(proj()).astype(o_ref.dtype)


def _proj0(x2, g, w_bf, *, tm=1024, tn=1024):
    t = x2.shape[0]
    n = w_bf.shape[1]
    return pl.pallas_call(
        functools.partial(_proj0_kernel, tiles_per_seg=A_WIDTH // tn),
        grid=(t // tm, n // tn),
        in_specs=[
            pl.BlockSpec((tm, D_MODEL), lambda i, j: (i, 0)),
            pl.BlockSpec((1, D_MODEL), lambda i, j: (0, 0)),
            pl.BlockSpec((D_MODEL, tn), lambda i, j: (0, j)),
        ],
        out_specs=pl.BlockSpec((tm, tn), lambda i, j: (i, j)),
        out_shape=jax.ShapeDtypeStruct((t, n), jnp.bfloat16),
        scratch_shapes=[pltpu.VMEM((tm, D_MODEL), jnp.bfloat16)],
        compiler_params=pltpu.CompilerParams(
            dimension_semantics=("arbitrary", "arbitrary"),
            vmem_limit_bytes=VMEM_LIMIT),
        name="proj0",
    )(x2, g, w_bf)


def _attn_kernel(lamp_ref, q_ref, k_ref, v_ref, za_ref, hn_ref, o_ref,
                 m_sc, l_sc, acc_sc, sa_sc, sb_sc, *, t):
    qi = pl.program_id(2)
    m_sc[...] = jnp.full(m_sc.shape, -jnp.inf, jnp.float32)
    l_sc[...] = jnp.zeros(l_sc.shape, jnp.float32)
    acc_sc[...] = jnp.zeros(acc_sc.shape, jnp.float32)
    nt = (((1,), (1,)), ((), ()))

    def scores(j, s_ref):
        ks = pl.multiple_of(j * t, t)
        s_ref[0:t, :] = lax.dot_general(
            q_ref[:, :A_HEAD_DIM], k_ref[pl.ds(ks, t), :A_HEAD_DIM], nt,
            preferred_element_type=jnp.float32)
        s_ref[t:2 * t, :] = lax.dot_general(
            q_ref[:, A_HEAD_DIM:], k_ref[pl.ds(ks, t), A_HEAD_DIM:], nt,
            preferred_element_type=jnp.float32)

    def softmax_pv(j, s_ref, masked):
        ks = pl.multiple_of(j * t, t)
        s = s_ref[...]
        if masked:
            rq = lax.broadcasted_iota(jnp.int32, (2 * t, 1), 0)
            rq = lax.shift_right_logical(jnp.where(rq >= t, rq - t, rq), CHUNK_SHIFT)
            ck = lax.shift_right_logical(
                lax.broadcasted_iota(jnp.int32, (1, t), 1), CHUNK_SHIFT)
            s = jnp.where(ck <= rq, s, -jnp.inf)
        m_prev = m_sc[...]
        m_new = jnp.maximum(m_prev, jnp.max(s, axis=1, keepdims=True))
        alpha = jnp.exp2(m_prev - m_new)
        p = jnp.exp2(s - pltpu.repeat(m_new, t // LANES, 1))
        l_sc[...] = alpha * l_sc[...] + jnp.sum(p, axis=1, keepdims=True)
        m_sc[...] = m_new
        pv = jnp.dot(p.astype(jnp.bfloat16), v_ref[pl.ds(ks, t), :],
                     preferred_element_type=jnp.float32)
        acc_sc[...] = acc_sc[...] * pltpu.repeat(alpha, A_VDIM // LANES, 1) + pv

    scores(0, sa_sc)

    def pair(p, c):
        j = 2 * p
        scores(j + 1, sb_sc)
        softmax_pv(j, sa_sc, False)
        scores(j + 2, sa_sc)
        softmax_pv(j + 1, sb_sc, False)
        return c

    lax.fori_loop(0, qi // 2, pair, 0)

    @pl.when(qi % 2 == 1)
    def _():
        scores(qi, sb_sc)
        softmax_pv(qi - 1, sa_sc, False)
        softmax_pv(qi, sb_sc, True)

    @pl.when(qi % 2 == 0)
    def _():
        softmax_pv(qi, sa_sc, True)

    lp = lamp_ref[...]
    d1 = jnp.sum(lp[0:1, :] * lp[1:2, :], axis=1, keepdims=True)
    d2 = jnp.sum(lp[2:3, :] * lp[3:4, :], axis=1, keepdims=True)
    lam = jnp.exp(d1) - jnp.exp(d2) + LAM_INIT_0

    inv_l = 1.0 / l_sc[...]
    acc = acc_sc[...] * pltpu.repeat(inv_l, A_VDIM // LANES, 1)
    o = acc[:t, :] - lam * acc[t:, :]
    ms = jnp.mean(o * o, axis=-1, keepdims=True)
    o = o * lax.rsqrt(ms + NORM_EPS) * hn_ref[...] * (1.0 - LAM_INIT_0)
    o_ref[...] = (o * za_ref[...].astype(jnp.float32)).astype(o_ref.dtype)


def _attention(proj3, lamp, hn, *, t=512):
    b, s, _ = proj3.shape
    nblk = A_WIDTH // A_VDIM
    return pl.pallas_call(
        functools.partial(_attn_kernel, t=t),
        grid=(b, A_HEADS, s // t),
        in_specs=[
            pl.BlockSpec((4, A_HEAD_DIM), lambda bi, h, qi: (0, 0)),
            pl.BlockSpec((None, t, A_VDIM), lambda bi, h, qi: (bi, qi, h)),
            pl.BlockSpec((None, s, A_VDIM), lambda bi, h, qi: (bi, 0, nblk + h)),
            pl.BlockSpec((None, s, A_VDIM), lambda bi, h, qi: (bi, 0, 2 * nblk + h)),
            pl.BlockSpec((None, t, A_VDIM), lambda bi, h, qi: (bi, qi, 3 * nblk + h)),
            pl.BlockSpec((1, A_VDIM), lambda bi, h, qi: (0, h)),
        ],
        out_specs=pl.BlockSpec((None, t, A_VDIM), lambda bi, h, qi: (bi, qi, h)),
        out_shape=jax.ShapeDtypeStruct((b, s, A_WIDTH), jnp.bfloat16),
        scratch_shapes=[
            pltpu.VMEM((2 * t, LANES), jnp.float32),
            pltpu.VMEM((2 * t, LANES), jnp.float32),
            pltpu.VMEM((2 * t, A_VDIM), jnp.float32),
            pltpu.VMEM((2 * t, t), jnp.float32),
            pltpu.VMEM((2 * t, t), jnp.float32),
        ],
        compiler_params=pltpu.CompilerParams(
            dimension_semantics=("arbitrary", "arbitrary", "arbitrary"),
            vmem_limit_bytes=VMEM_LIMIT),
        name="diff_attn",
    )(lamp, proj3, proj3, proj3, proj3, hn)


def _sgu_kernel(gu_ref, gv_ref, zb_ref, lng_ref, lnb_ref, ws_ref, bst_ref,
                o_ref, *, rows):
    v = gv_ref[...].astype(jnp.float32)
    mu = jnp.mean(v, axis=-1, keepdims=True)
    vc = v - mu
    var = jnp.mean(vc * vc, axis=-1, keepdims=True)
    vln = (vc * lax.rsqrt(var + NORM_EPS) * lng_ref[...] + lnb_ref[...])
    vln = vln.astype(jnp.bfloat16)
    ci = lax.broadcasted_iota(jnp.int32, (B_SPAN, B_SPAN), 0) // CHUNK
    cj = lax.broadcasted_iota(jnp.int32, (B_SPAN, B_SPAN), 1) // CHUNK
    keep = ci >= cj
    for g in range(B_GROUPS):
        w = jnp.where(keep, ws_ref[g], 0.0).astype(jnp.bfloat16)
        bcol = bst_ref[:, g:g + 1]
        c0 = g * B_GROUP_DIM
        for sp in range(rows // B_SPAN):
            r0 = sp * B_SPAN
            mixed = jnp.dot(w, vln[r0:r0 + B_SPAN, c0:c0 + B_GROUP_DIM],
                            preferred_element_type=jnp.float32) + bcol
            u = gu_ref[r0:r0 + B_SPAN, c0:c0 + B_GROUP_DIM].astype(jnp.float32)
            z = zb_ref[r0:r0 + B_SPAN, c0:c0 + B_GROUP_DIM].astype(jnp.float32)
            o_ref[r0:r0 + B_SPAN, c0:c0 + B_GROUP_DIM] = (u * mixed * z).astype(o_ref.dtype)


def _sgu(proj, ln_g, ln_b, w_s, b_st, *, rows=256):
    t = proj.shape[0]
    base = 4 * A_WIDTH // B_WIDTH
    return pl.pallas_call(
        functools.partial(_sgu_kernel, rows=rows),
        grid=(t // rows,),
        in_specs=[
            pl.BlockSpec((rows, B_WIDTH), lambda i: (i, base)),
            pl.BlockSpec((rows, B_WIDTH), lambda i: (i, base + 1)),
            pl.BlockSpec((rows, B_WIDTH), lambda i: (i, base + 2)),
            pl.BlockSpec((1, B_WIDTH), lambda i: (0, 0)),
            pl.BlockSpec((1, B_WIDTH), lambda i: (0, 0)),
            pl.BlockSpec((B_GROUPS, B_SPAN, B_SPAN), lambda i: (0, 0, 0)),
            pl.BlockSpec((B_SPAN, B_GROUPS), lambda i: (0, 0)),
        ],
        out_specs=pl.BlockSpec((rows, B_WIDTH), lambda i: (i, 0)),
        out_shape=jax.ShapeDtypeStruct((t, B_WIDTH), jnp.bfloat16),
        compiler_params=pltpu.CompilerParams(
            dimension_semantics=("arbitrary",),
            vmem_limit_bytes=VMEM_LIMIT),
        name="sgu",
    )(proj, proj, proj, ln_g, ln_b, w_s, b_st)


def _out0_kernel(ya_ref, yb_ref, x_ref, wa_ref, wb_ref, g_ref, x1_ref, xn_ref):
    y = jnp.dot(ya_ref[...], wa_ref[...], preferred_element_type=jnp.float32)
    y = y + jnp.dot(yb_ref[...], wb_ref[...], preferred_element_type=jnp.float32)
    x1 = x_ref[...] + y
    x1_ref[...] = x1
    xn_ref[...] = _rms_rows(x1, g_ref[...]).astype(xn_ref.dtype)


def _out0(ya, yb, x2, w_bf, g, *, tm=256):
    t = x2.shape[0]
    const = pl.Buffered(1)
    return pl.pallas_call(
        _out0_kernel,
        grid=(t // tm,),
        in_specs=[
            pl.BlockSpec((tm, A_WIDTH), lambda i: (i, 0)),
            pl.BlockSpec((tm, B_WIDTH), lambda i: (i, 0)),
            pl.BlockSpec((tm, D_MODEL), lambda i: (i, 0)),
            pl.BlockSpec((A_WIDTH, D_MODEL), lambda i: (0, 0), pipeline_mode=const),
            pl.BlockSpec((B_WIDTH, D_MODEL), lambda i: (1, 0), pipeline_mode=const),
            pl.BlockSpec((1, D_MODEL), lambda i: (0, 0)),
        ],
        out_specs=[
            pl.BlockSpec((tm, D_MODEL), lambda i: (i, 0)),
            pl.BlockSpec((tm, D_MODEL), lambda i: (i, 0)),
        ],
        out_shape=[
            jax.ShapeDtypeStruct((t, D_MODEL), jnp.float32),
            jax.ShapeDtypeStruct((t, D_MODEL), jnp.bfloat16),
        ],
        compiler_params=pltpu.CompilerParams(
            dimension_semantics=("arbitrary",),
            vmem_limit_bytes=VMEM_LIMIT),
        name="out0",
    )(ya, yb, x2, w_bf, w_bf, g)


def _proj1_kernel(xn_ref, w_ref, o_ref):
    def proj():
        return jnp.dot(xn_ref[...], w_ref[...], preferred_element_type=jnp.float32)

    @pl.when(pl.program_id(0) == 0)
    def _():
        o_ref[...] = proj().astype(o_ref.dtype)

    @pl.when(pl.program_id(0) == 1)
    def _():
        o_ref[...] = _silu(proj()).astype(o_ref.dtype)


def _proj1(xn, w_bf, *, tm=512):
    t = xn.shape[0]
    return pl.pallas_call(
        _proj1_kernel,
        grid=(2, t // tm),
        in_specs=[
            pl.BlockSpec((tm, D_MODEL), lambda j, i: (i, 0)),
            pl.BlockSpec((D_MODEL, C_WIDTH), lambda j, i: (0, j)),
        ],
        out_specs=pl.BlockSpec((tm, C_WIDTH), lambda j, i: (i, j)),
        out_shape=jax.ShapeDtypeStruct((t, 2 * C_WIDTH), jnp.bfloat16),
        compiler_params=pltpu.CompilerParams(
            dimension_semantics=("arbitrary", "arbitrary"),
            vmem_limit_bytes=VMEM_LIMIT),
        name="proj1",
    )(xn, w_bf)


def _segment_pitch(seg):
    units = seg // SUBLANES
    return SUBLANES * (units + 1 if units % 2 == 0 else units)


def _build_gate_slabs(wp_ref, wg_sc, tmp_sc):
    row = lax.broadcasted_iota(jnp.int32, (GATE_PAD, LANES), 0)
    col = lax.broadcasted_iota(jnp.int32, (GATE_PAD, LANES), 1)
    for j in range(N_CT):
        tmp_sc[...] = jnp.zeros(tmp_sc.shape, jnp.float32)
        h_lo = (LANES * j) // C_BLOCK
        h_hi = (LANES * j + LANES - 1) // C_BLOCK
        for gate in range(2):
            for h in range(h_lo, h_hi + 1):
                sel = jnp.where(row == col + (LANES * j - C_BLOCK * h), 1.0, 0.0)
                part = jnp.dot(wp_ref[gate, h], sel.astype(jnp.bfloat16),
                               preferred_element_type=jnp.float32)
                r0 = C_BLOCK * h - GATE_WS[j]
                cs = slice(gate * LANES, (gate + 1) * LANES)
                tmp_sc[r0:r0 + C_BLOCK, cs] = tmp_sc[r0:r0 + C_BLOCK, cs] + part[:C_BLOCK, :]
        wg_sc[j] = (0.5 * tmp_sc[...]).astype(jnp.bfloat16)


def _rglru_kernel(xb_ref, zc_ref, cw_ref, cb_ref, wp_ref, ba_ref, bx_ref,
                  lam_ref, o_ref, pad_sc, xc_sc, xcb_sc, ga_sc, gx_sc, h_sc,
                  wg_ref, tmp_sc, hl_sc, p_sc, prm_sc, *, ts, rows):
    si = pl.program_id(1)
    halo = SUBLANES
    vshape = (N_CT, SUBLANES, LANES)

    @pl.when((pl.program_id(0) == 0) & (si == 0))
    def _():
        _build_gate_slabs(wp_ref, wg_ref, tmp_sc)
        lam = lam_ref[...]
        sp = jnp.maximum(-lam, 0.0) + jnp.log1p(jnp.exp(-jnp.abs(lam)))
        prm_sc[0] = jnp.broadcast_to((-0.5 * C_GATE_C) * sp, vshape)
        prm_sc[1] = jnp.broadcast_to(0.5 * ba_ref[...], vshape)
        prm_sc[2] = jnp.broadcast_to(0.5 * bx_ref[...], vshape)

    @pl.when(si == 0)
    def _():
        pad_sc[0:halo, :] = jnp.zeros((halo, C_WIDTH), jnp.float32)
        h_sc[...] = jnp.zeros(h_sc.shape, jnp.float32)

    @pl.when(si > 0)
    def _():
        pad_sc[0:halo, :] = pad_sc[ts:ts + halo, :]

    pad_sc[halo:halo + ts, :] = xb_ref[...].astype(jnp.float32)
    xc = cb_ref[...] + cw_ref[C_CONV - 1:C_CONV, :] * pad_sc[halo:halo + ts, :]
    for tap in range(C_CONV - 1):
        off = halo - (C_CONV - 1) + tap
        xc = xc + cw_ref[tap:tap + 1, :] * pad_sc[off:off + ts, :]
    xcb_sc[...] = xc.astype(jnp.bfloat16)
    seg = ts // SUBLANES
    pitch = _segment_pitch(seg)

    def put(dst_sc, j, val):
        for s in range(SUBLANES):
            dst_sc[j, s * pitch:s * pitch + seg, :] = val[s * seg:(s + 1) * seg, :]

    for j in range(N_CT):
        put(xc_sc, j, xc[:, j * LANES:(j + 1) * LANES])

    for j in range(N_CT):
        ws = GATE_WS[j]
        g = jnp.dot(xcb_sc[:, ws:ws + GATE_WIN], wg_ref[j],
                    preferred_element_type=jnp.float32)
        put(ga_sc, j, g[:, :LANES])
        put(gx_sc, j, g[:, LANES:])

    blocks_per_seg = seg // rows

    def padded_rows(it):
        start = (it // blocks_per_seg) * pitch + (it % blocks_per_seg) * rows
        return pl.ds(pl.multiple_of(start, SUBLANES), rows)

    def local(i, carry):
        h, p = carry
        rs = pl.ds(i, SUBLANES, stride=pitch)
        half_c = prm_sc[0]
        tr = jnp.tanh(ga_sc[:, rs, :] + prm_sc[1])
        ti = jnp.tanh(gx_sc[:, rs, :] + prm_sc[2])
        log_a = half_c + half_c * tr
        a = jnp.exp(log_a)
        half_a = 0.5 * a
        y = (-0.25 - half_a * half_a) * jnp.tanh(log_a)
        half_beta = jnp.where(y > 0.0, y * lax.rsqrt(y), 0.0)
        x = xc_sc[:, rs, :]
        b = half_beta * (x + x * ti)
        h = a * h + b
        p = a * p
        blk = pl.ds(pl.multiple_of(i * SUBLANES, SUBLANES), SUBLANES)
        hl_sc[:, blk, :] = h
        p_sc[:, blk, :] = p
        return h, p

    h_end, p_end = lax.fori_loop(
        0, seg, local,
        (jnp.zeros(vshape, jnp.float32), jnp.ones(vshape, jnp.float32)), unroll=2)

    sub = lax.broadcasted_iota(jnp.int32, vshape, 1)
    a, b = p_end, h_end
    for d in (1, 2, 4):
        keep = sub >= d
        a_sh = pltpu.roll(a, d, 1)
        b_sh = pltpu.roll(b, d, 1)
        b = jnp.where(keep, a * b_sh + b, b)
        a = jnp.where(keep, a * a_sh, a)
    carry_in = h_sc[...]
    seg_end = a * carry_in + b
    h_in = jnp.where(sub == 0, carry_in, pltpu.roll(seg_end, 1, 1))
    h_sc[...] = jnp.broadcast_to(seg_end[:, SUBLANES - 1:SUBLANES, :], vshape)

    def fix(i, c):
        blk = pl.ds(pl.multiple_of(i * SUBLANES, SUBLANES), SUBLANES)
        xc_sc[:, pl.ds(i, SUBLANES, stride=pitch), :] = hl_sc[:, blk, :] + p_sc[:, blk, :] * h_in
        return c

    lax.fori_loop(0, seg, fix, 0)

    def emit(it, c):
        rs = pl.ds(pl.multiple_of(it * rows, rows), rows)
        ps = padded_rows(it)
        h = jnp.concatenate([xc_sc[j, ps, :] for j in range(N_CT)], axis=-1)
        o_ref[rs, :] = (h * zc_ref[rs, :].astype(jnp.float32)).astype(o_ref.dtype)
        return c

    lax.fori_loop(0, ts // rows, emit, 0)


def _rglru(xbz3, cw, cb, wp, ba, bx, lam, *, ts=256, rows=16):
    b, s, _ = xbz3.shape
    const = pl.Buffered(1)
    vec = lambda: pl.BlockSpec((N_CT, 1, LANES), lambda bi, si: (0, 0, 0))
    tiled = lambda: pltpu.VMEM((N_CT, ts, LANES), jnp.float32)
    padded = lambda: pltpu.VMEM(
        (N_CT, SUBLANES * _segment_pitch(ts // SUBLANES), LANES), jnp.float32)
    tile_major = lambda v: v.reshape(N_CT, 1, LANES)
    return pl.pallas_call(
        functools.partial(_rglru_kernel, ts=ts, rows=rows),
        grid=(b, s // ts),
        in_specs=[
            pl.BlockSpec((None, ts, C_WIDTH), lambda bi, si: (bi, si, 0)),
            pl.BlockSpec((None, ts, C_WIDTH), lambda bi, si: (bi, si, 1)),
            pl.BlockSpec((C_CONV, C_WIDTH), lambda bi, si: (0, 0)),
            pl.BlockSpec((1, C_WIDTH), lambda bi, si: (0, 0)),
            pl.BlockSpec((2, C_HEADS, GATE_PAD, GATE_PAD), lambda bi, si: (0, 0, 0, 0),
                         pipeline_mode=const),
            vec(), vec(), vec(),
        ],
        out_specs=pl.BlockSpec((None, ts, C_WIDTH), lambda bi, si: (bi, si, 0)),
        out_shape=jax.ShapeDtypeStruct((b, s, C_WIDTH), jnp.bfloat16),
        scratch_shapes=[
            pltpu.VMEM((ts + SUBLANES, C_WIDTH), jnp.float32),
            padded(),
            pltpu.VMEM((ts, C_WIDTH), jnp.bfloat16),
            padded(),
            padded(),
            pltpu.VMEM((N_CT, SUBLANES, LANES), jnp.float32),
            pltpu.VMEM((N_CT, GATE_WIN, 2 * LANES), jnp.bfloat16),
            pltpu.VMEM((GATE_WIN, 2 * LANES), jnp.float32),
            tiled(),
            tiled(),
            pltpu.VMEM((3, N_CT, SUBLANES, LANES), jnp.float32),
        ],
        compiler_params=pltpu.CompilerParams(
            dimension_semantics=("arbitrary", "arbitrary"),
            vmem_limit_bytes=VMEM_LIMIT),
        name="rglru",
    )(xbz3, xbz3, cw, cb, wp, tile_major(ba), tile_major(bx), tile_major(lam))


def _out1_kernel(hz_ref, x_ref, w_ref, g_ref, o_ref):
    y = jnp.dot(hz_ref[...], w_ref[...], preferred_element_type=jnp.float32)
    o_ref[...] = _rms_rows(x_ref[...] + y, g_ref[...])


def _out1(hz, x1, w_bf, g, *, tm=256):
    t = x1.shape[0]
    return pl.pallas_call(
        _out1_kernel,
        grid=(t // tm,),
        in_specs=[
            pl.BlockSpec((tm, C_WIDTH), lambda i: (i, 0)),
            pl.BlockSpec((tm, D_MODEL), lambda i: (i, 0)),
            pl.BlockSpec((C_WIDTH, D_MODEL), lambda i: (0, 0),
                         pipeline_mode=pl.Buffered(1)),
            pl.BlockSpec((1, D_MODEL), lambda i: (0, 0)),
        ],
        out_specs=pl.BlockSpec((tm, D_MODEL), lambda i: (i, 0)),
        out_shape=jax.ShapeDtypeStruct((t, D_MODEL), jnp.float32),
        compiler_params=pltpu.CompilerParams(
            dimension_semantics=("arbitrary",),
            vmem_limit_bytes=VMEM_LIMIT),
        name="out1",
    )(hz, x1, w_bf, g)


def kernel(x, ab_norm, ab_w_in, ab_lambda, ab_head_norm, ab_sgu_ln_g, ab_sgu_ln_b, ab_sgu_w, ab_sgu_b, ab_w_out, c_norm, c_w_in, c_conv_w, c_conv_b, c_gate_a_w, c_gate_a_b, c_gate_x_w, c_gate_x_b, c_lambda, c_w_out, final_norm):
    bsz, seq, d = x.shape
    t = bsz * seq
    bf = jnp.bfloat16
    row = lambda v: v.reshape(1, -1)
    x2 = x.reshape(t, d)

    proj = _proj0(x2, row(ab_norm[0]), ab_w_in[0].astype(bf))
    ya = _attention(proj.reshape(bsz, seq, AB_IN), ab_lambda[0], row(ab_head_norm[0]))
    yb = _sgu(proj, row(ab_sgu_ln_g[0]), row(ab_sgu_ln_b[0]), ab_sgu_w[0], ab_sgu_b[0].T)
    x1, xn1 = _out0(ya.reshape(t, A_WIDTH), yb, x2, ab_w_out[0].astype(bf), row(c_norm[0]))

    xbz = _proj1(xn1, c_w_in[0].astype(bf))
    padw = GATE_PAD - C_BLOCK
    wp = jnp.pad(jnp.stack([c_gate_a_w[0], c_gate_x_w[0]]).astype(bf),
                 ((0, 0), (0, 0), (0, padw), (0, padw)))
    hz = _rglru(xbz.reshape(bsz, seq, 2 * C_WIDTH), c_conv_w[0], row(c_conv_b[0]), wp,
                row(c_gate_a_b[0]), row(c_gate_x_b[0]), row(c_lambda[0]))
    out = _out1(hz.reshape(t, C_WIDTH), x1, c_w_out[0].astype(bf), row(final_norm))
    return out.reshape(bsz, seq, d)
```

```python
import functools
import math

import jax
import jax.numpy as jnp
import numpy as np
from jax import lax
from jax.experimental import pallas as pl
from jax.experimental.pallas import tpu as pltpu

D_MODEL = 2048
CHUNK = 64
CHUNK_SHIFT = 6
NORM_EPS = 1e-6
A_HEADS = 8
A_HEAD_DIM = 128
A_VDIM = 2 * A_HEAD_DIM
A_WIDTH = A_HEADS * A_VDIM
B_GROUPS = 8
B_WIDTH = D_MODEL
B_GROUP_DIM = B_WIDTH // B_GROUPS
B_SPAN = 128
C_WIDTH = 2688
C_HEADS = 16
C_BLOCK = C_WIDTH // C_HEADS
C_CONV = 4
C_GATE_C = 8.0
AB_IN = 4 * A_WIDTH + 3 * B_WIDTH
LAM_INIT_0 = 0.8 - 0.6 * math.exp(-0.3 * 0)

LANES = 128
SUBLANES = 8
LOG2E = 1.4426950408889634
VMEM_LIMIT = 56 * 1024 * 1024

GATE_WIN = 512
GATE_PAD = 256
N_CT = C_WIDTH // LANES


def _gate_window_start(j):
    h0 = (LANES * j) // C_BLOCK
    ws = (C_BLOCK * h0) // LANES * LANES
    return min(ws, C_WIDTH - GATE_WIN)


GATE_WS = tuple(_gate_window_start(j) for j in range(N_CT))


def _silu(z):
    return z * (1.0 / (1.0 + jnp.exp(-z)))


def _gelu_tanh(x):
    c = math.sqrt(2.0 / math.pi)
    return 0.5 * x * (1.0 + jnp.tanh(c * (x + 0.044715 * (x * x * x))))


def _rms_rows(xf, g):
    ms = jnp.mean(xf * xf, axis=-1, keepdims=True)
    return xf * lax.rsqrt(ms + NORM_EPS) * g


def _proj0_kernel(x_ref, g_ref, w_ref, o_ref, xn_sc, *, tiles_per_seg):
    j = pl.program_id(1)

    @pl.when(j == 0)
    def _():
        xn_sc[...] = _rms_rows(x_ref[...], g_ref[...]).astype(xn_sc.dtype)

    seg = j // tiles_per_seg

    def proj():
        return jnp.dot(xn_sc[...], w_ref[...], preferred_element_type=jnp.float32)

    @pl.when(seg <= 2)
    def _():
        scale = jnp.where(seg == 0, A_HEAD_DIM ** -0.5 * LOG2E, 1.0).astype(jnp.float32)
        o_ref[...] = (proj() * scale).astype(o_ref.dtype)

    @pl.when((seg == 3) | (seg == 6))
    def _():
        o_ref[...] = _silu(proj()).astype(o_ref.dtype)

    @pl.when((seg == 4) | (seg == 5))
    def _():
        o_ref[...] = _gelu_tanh(proj()).astype(o_ref.dtype)


def _proj0(x2, g, w_bf, *, tm=1024, tn=1024):
    t = x2.shape[0]
    n = w_bf.shape[1]
    return pl.pallas_call(
        functools.partial(_proj0_kernel, tiles_per_seg=A_WIDTH // tn),
        grid=(t // tm, n // tn),
        in_specs=[
            pl.BlockSpec((tm, D_MODEL), lambda i, j: (i, 0)),
            pl.BlockSpec((1, D_MODEL), lambda i, j: (0, 0)),
            pl.BlockSpec((D_MODEL, tn), lambda i, j: (0, j)),
        ],
        out_specs=pl.BlockSpec((tm, tn), lambda i, j: (i, j)),
        out_shape=jax.ShapeDtypeStruct((t, n), jnp.bfloat16),
        scratch_shapes=[pltpu.VMEM((tm, D_MODEL), w_bf.dtype)],
        compiler_params=pltpu.CompilerParams(
            dimension_semantics=("arbitrary", "arbitrary"),
            vmem_limit_bytes=VMEM_LIMIT),
        name="proj0",
    )(x2, g, w_bf)


def _attn_kernel(lamp_ref, q_ref, k_ref, v_ref, za_ref, hn_ref, o_ref,
                 m_sc, l_sc, acc_sc, sa_sc, sb_sc, *, t):
    qi = pl.program_id(2)
    m_sc[...] = jnp.full(m_sc.shape, -jnp.inf, jnp.float32)
    l_sc[...] = jnp.zeros(l_sc.shape, jnp.float32)
    acc_sc[...] = jnp.zeros(acc_sc.shape, jnp.float32)
    nt = (((1,), (1,)), ((), ()))

    def scores(j, s_ref):
        ks = pl.multiple_of(j * t, t)
        s_ref[0:t, :] = lax.dot_general(
            q_ref[:, :A_HEAD_DIM], k_ref[pl.ds(ks, t), :A_HEAD_DIM], nt,
            preferred_element_type=jnp.float32)
        s_ref[t:2 * t, :] = lax.dot_general(
            q_ref[:, A_HEAD_DIM:], k_ref[pl.ds(ks, t), A_HEAD_DIM:], nt,
            preferred_element_type=jnp.float32)

    def softmax_pv(j, s_ref, masked):
        ks = pl.multiple_of(j * t, t)
        s = s_ref[...]
        if masked:
            rq = lax.broadcasted_iota(jnp.int32, (2 * t, 1), 0)
            rq = lax.shift_right_logical(jnp.where(rq >= t, rq - t, rq), CHUNK_SHIFT)
            ck = lax.shift_right_logical(
                lax.broadcasted_iota(jnp.int32, (1, t), 1), CHUNK_SHIFT)
            s = jnp.where(ck <= rq, s, -jnp.inf)
        m_prev = m_sc[...]
        m_new = jnp.maximum(m_prev, jnp.max(s, axis=1, keepdims=True))
        alpha = jnp.exp2(m_prev - m_new)
        p = jnp.exp2(s - pltpu.repeat(m_new, t // LANES, 1))
        l_sc[...] = alpha * l_sc[...] + jnp.sum(p, axis=1, keepdims=True)
        m_sc[...] = m_new
        pv = jnp.dot(p.astype(jnp.bfloat16), v_ref[pl.ds(ks, t), :],
                     preferred_element_type=jnp.float32)
        acc_sc[...] = acc_sc[...] * pltpu.repeat(alpha, A_VDIM // LANES, 1) + pv

    scores(0, sa_sc)

    def pair(p, c):
        j = 2 * p
        scores(j + 1, sb_sc)
        softmax_pv(j, sa_sc, False)
        scores(j + 2, sa_sc)
        softmax_pv(j + 1, sb_sc, False)
        return c

    lax.fori_loop(0, qi // 2, pair, 0)

    @pl.when(qi % 2 == 1)
    def _():
        scores(qi, sb_sc)
        softmax_pv(qi - 1, sa_sc, False)
        softmax_pv(qi, sb_sc, True)

    @pl.when(qi % 2 == 0)
    def _():
        softmax_pv(qi, sa_sc, True)

    lp = lamp_ref[...]
    d1 = jnp.sum(lp[0:1, :] * lp[1:2, :], axis=1, keepdims=True)
    d2 = jnp.sum(lp[2:3, :] * lp[3:4, :], axis=1, keepdims=True)
    lam = jnp.exp(d1) - jnp.exp(d2) + LAM_INIT_0

    inv_l = 1.0 / l_sc[...]
    acc = acc_sc[...] * pltpu.repeat(inv_l, A_VDIM // LANES, 1)
    o = acc[:t, :] - lam * acc[t:, :]
    ms = jnp.mean(o * o, axis=-1, keepdims=True)
    o = o * lax.rsqrt(ms + NORM_EPS) * hn_ref[...] * (1.0 - LAM_INIT_0)
    o_ref[...] = (o * za_ref[...].astype(jnp.float32)).astype(o_ref.dtype)


def _attention(proj3, lamp, hn, *, t=512):
    b, s, _ = proj3.shape
    nblk = A_WIDTH // A_VDIM
    return pl.pallas_call(
        functools.partial(_attn_kernel, t=t),
        grid=(b, A_HEADS, s // t),
        in_specs=[
            pl.BlockSpec((4, A_HEAD_DIM), lambda bi, h, qi: (0, 0)),
            pl.BlockSpec((None, t, A_VDIM), lambda bi, h, qi: (bi, qi, h)),
            pl.BlockSpec((None, s, A_VDIM), lambda bi, h, qi: (bi, 0, nblk + h)),
            pl.BlockSpec((None, s, A_VDIM), lambda bi, h, qi: (bi, 0, 2 * nblk + h)),
            pl.BlockSpec((None, t, A_VDIM), lambda bi, h, qi: (bi, qi, 3 * nblk + h)),
            pl.BlockSpec((1, A_VDIM), lambda bi, h, qi: (0, h)),
        ],
        out_specs=pl.BlockSpec((None, t, A_VDIM), lambda bi, h, qi: (bi, qi, h)),
        out_shape=jax.ShapeDtypeStruct((b, s, A_WIDTH), jnp.bfloat16),
        scratch_shapes=[
            pltpu.VMEM((2 * t, LANES), jnp.float32),
            pltpu.VMEM((2 * t, LANES), jnp.float32),
            pltpu.VMEM((2 * t, A_VDIM), jnp.float32),
            pltpu.VMEM((2 * t, t), jnp.float32),
            pltpu.VMEM((2 * t, t), jnp.float32),
        ],
        compiler_params=pltpu.CompilerParams(
            dimension_semantics=("arbitrary", "arbitrary", "arbitrary"),
            vmem_limit_bytes=VMEM_LIMIT),
        name="diff_attn",
    )(lamp, proj3, proj3, proj3, proj3, hn)


def _sgu_tile(gu_ref, gv_ref, zb_ref, lng_ref, lnb_ref, ws_ref, bst_ref, yb_sc, rows):
    v = gv_ref[...].astype(jnp.float32)
    mu = jnp.mean(v, axis=-1, keepdims=True)
    vc = v - mu
    var = jnp.mean(vc * vc, axis=-1, keepdims=True)
    vln = (vc * lax.rsqrt(var + NORM_EPS) * lng_ref[...] + lnb_ref[...])
    vln = vln.astype(jnp.bfloat16)
    ci = lax.broadcasted_iota(jnp.int32, (B_SPAN, B_SPAN), 0) // CHUNK
    cj = lax.broadcasted_iota(jnp.int32, (B_SPAN, B_SPAN), 1) // CHUNK
    keep = ci >= cj
    for g in range(B_GROUPS):
        w = jnp.where(keep, ws_ref[g], 0.0).astype(jnp.bfloat16)
        bcol = bst_ref[:, g:g + 1]
        c0 = g * B_GROUP_DIM
        for sp in range(rows // B_SPAN):
            r0 = sp * B_SPAN
            mixed = jnp.dot(w, vln[r0:r0 + B_SPAN, c0:c0 + B_GROUP_DIM],
                            preferred_element_type=jnp.float32) + bcol
            u = gu_ref[r0:r0 + B_SPAN, c0:c0 + B_GROUP_DIM].astype(jnp.float32)
            z = zb_ref[r0:r0 + B_SPAN, c0:c0 + B_GROUP_DIM].astype(jnp.float32)
            yb_sc[r0:r0 + B_SPAN, c0:c0 + B_GROUP_DIM] = (u * mixed * z).astype(yb_sc.dtype)


def _out0_kernel(ya_ref, gu_ref, gv_ref, zb_ref, x_ref, wa_ref, wb_ref, g_ref,
                 lng_ref, lnb_ref, ws_ref, bst_ref, x1_ref, xn_ref, yb_sc, *, rows):
    y = jnp.dot(ya_ref[...], wa_ref[...], preferred_element_type=jnp.float32)
    _sgu_tile(gu_ref, gv_ref, zb_ref, lng_ref, lnb_ref, ws_ref, bst_ref, yb_sc, rows)
    y = y + jnp.dot(yb_sc[...], wb_ref[...], preferred_element_type=jnp.float32)
    x1 = x_ref[...] + y
    x1_ref[...] = x1
    xn_ref[...] = _rms_rows(x1, g_ref[...]).astype(xn_ref.dtype)


def _out0(ya, proj, x2, w_bf, g, ln_g, ln_b, w_s, b_st, *, tm=256):
    t = x2.shape[0]
    const = pl.Buffered(1)
    base = 4 * A_WIDTH // B_WIDTH
    vec = lambda n: pl.BlockSpec((1, n), lambda i: (0, 0))
    return pl.pallas_call(
        functools.partial(_out0_kernel, rows=tm),
        grid=(t // tm,),
        in_specs=[
            pl.BlockSpec((tm, A_WIDTH), lambda i: (i, 0)),
            pl.BlockSpec((tm, B_WIDTH), lambda i: (i, base)),
            pl.BlockSpec((tm, B_WIDTH), lambda i: (i, base + 1)),
            pl.BlockSpec((tm, B_WIDTH), lambda i: (i, base + 2)),
            pl.BlockSpec((tm, D_MODEL), lambda i: (i, 0)),
            pl.BlockSpec((A_WIDTH, D_MODEL), lambda i: (0, 0), pipeline_mode=const),
            pl.BlockSpec((B_WIDTH, D_MODEL), lambda i: (1, 0), pipeline_mode=const),
            vec(D_MODEL), vec(B_WIDTH), vec(B_WIDTH),
            pl.BlockSpec((B_GROUPS, B_SPAN, B_SPAN), lambda i: (0, 0, 0)),
            pl.BlockSpec((B_SPAN, B_GROUPS), lambda i: (0, 0)),
        ],
        out_specs=[
            pl.BlockSpec((tm, D_MODEL), lambda i: (i, 0)),
            pl.BlockSpec((tm, D_MODEL), lambda i: (i, 0)),
        ],
        out_shape=[
            jax.ShapeDtypeStruct((t, D_MODEL), jnp.float32),
            jax.ShapeDtypeStruct((t, D_MODEL), jnp.bfloat16),
        ],
        scratch_shapes=[pltpu.VMEM((tm, B_WIDTH), jnp.bfloat16)],
        compiler_params=pltpu.CompilerParams(
            dimension_semantics=("arbitrary",),
            vmem_limit_bytes=VMEM_LIMIT),
        name="out0",
    )(ya, proj, proj, proj, x2, w_bf, w_bf, g, ln_g, ln_b, w_s, b_st)


def _proj1_kernel(xn_ref, w_ref, o_ref):
    def proj():
        return jnp.dot(xn_ref[...].astype(w_ref.dtype), w_ref[...],
                       preferred_element_type=jnp.float32)

    @pl.when(pl.program_id(0) == 0)
    def _():
        o_ref[...] = proj().astype(o_ref.dtype)

    @pl.when(pl.program_id(0) == 1)
    def _():
        o_ref[...] = _silu(proj()).astype(o_ref.dtype)


def _proj1(xn, w_bf, *, tm=512):
    t = xn.shape[0]
    return pl.pallas_call(
        _proj1_kernel,
        grid=(2, t // tm),
        in_specs=[
            pl.BlockSpec((tm, D_MODEL), lambda j, i: (i, 0)),
            pl.BlockSpec((D_MODEL, C_WIDTH), lambda j, i: (0, j),
                         pipeline_mode=pl.Buffered(1)),
        ],
        out_specs=pl.BlockSpec((tm, C_WIDTH), lambda j, i: (i, j)),
        out_shape=jax.ShapeDtypeStruct((t, 2 * C_WIDTH), jnp.bfloat16),
        compiler_params=pltpu.CompilerParams(
            dimension_semantics=("arbitrary", "arbitrary"),
            vmem_limit_bytes=VMEM_LIMIT),
        name="proj1",
    )(xn, w_bf)


def _segment_pitch(seg):
    units = seg // SUBLANES
    return SUBLANES * (units + 1 if units % 2 == 0 else units)


def _build_gate_slabs(wp_ref, wg_sc, tmp_sc):
    row = lax.broadcasted_iota(jnp.int32, (GATE_PAD, LANES), 0)
    col = lax.broadcasted_iota(jnp.int32, (GATE_PAD, LANES), 1)
    for j in range(N_CT):
        tmp_sc[...] = jnp.zeros(tmp_sc.shape, jnp.float32)
        h_lo = (LANES * j) // C_BLOCK
        h_hi = (LANES * j + LANES - 1) // C_BLOCK
        for gate in range(2):
            for h in range(h_lo, h_hi + 1):
                sel = jnp.where(row == col + (LANES * j - C_BLOCK * h), 1.0, 0.0)
                part = jnp.dot(wp_ref[gate, h], sel.astype(jnp.bfloat16),
                               preferred_element_type=jnp.float32)
                r0 = C_BLOCK * h - GATE_WS[j]
                cs = slice(gate * LANES, (gate + 1) * LANES)
                tmp_sc[r0:r0 + C_BLOCK, cs] = tmp_sc[r0:r0 + C_BLOCK, cs] + part[:C_BLOCK, :]
        wg_sc[j] = (0.5 * tmp_sc[...]).astype(jnp.bfloat16)


def _rglru_kernel(xb_ref, zc_ref, cw_ref, cb_ref, wp_ref, ba_ref, bx_ref,
                  lam_ref, o_ref, pad_sc, xc_sc, xcb_sc, ga_sc, gx_sc, h_sc,
                  wg_ref, tmp_sc, hl_sc, p_sc, prm_sc, *, ts, rows):
    si = pl.program_id(1)
    halo = SUBLANES
    vshape = (N_CT, SUBLANES, LANES)

    @pl.when((pl.program_id(0) == 0) & (si == 0))
    def _():
        _build_gate_slabs(wp_ref, wg_ref, tmp_sc)
        lam = lam_ref[...]
        sp = jnp.maximum(-lam, 0.0) + jnp.log1p(jnp.exp(-jnp.abs(lam)))
        prm_sc[0] = jnp.broadcast_to((-0.5 * C_GATE_C) * sp, vshape)
        prm_sc[1] = jnp.broadcast_to(0.5 * ba_ref[...], vshape)
        prm_sc[2] = jnp.broadcast_to(0.5 * bx_ref[...], vshape)

    @pl.when(si == 0)
    def _():
        pad_sc[0:halo, :] = jnp.zeros((halo, C_WIDTH), jnp.float32)
        h_sc[...] = jnp.zeros(h_sc.shape, jnp.float32)

    @pl.when(si > 0)
    def _():
        pad_sc[0:halo, :] = pad_sc[ts:ts + halo, :]

    pad_sc[halo:halo + ts, :] = xb_ref[...].astype(jnp.float32)
    xc = cb_ref[...] + cw_ref[C_CONV - 1:C_CONV, :] * pad_sc[halo:halo + ts, :]
    for tap in range(C_CONV - 1):
        off = halo - (C_CONV - 1) + tap
        xc = xc + cw_ref[tap:tap + 1, :] * pad_sc[off:off + ts, :]
    xcb_sc[...] = xc.astype(jnp.bfloat16)
    seg = ts // SUBLANES
    pitch = _segment_pitch(seg)

    def put(dst_sc, j, val):
        for s in range(SUBLANES):
            dst_sc[j, s * pitch:s * pitch + seg, :] = val[s * seg:(s + 1) * seg, :]

    for j in range(N_CT):
        put(xc_sc, j, xc[:, j * LANES:(j + 1) * LANES])

    for j in range(N_CT):
        ws = GATE_WS[j]
        g = jnp.dot(xcb_sc[:, ws:ws + GATE_WIN], wg_ref[j],
                    preferred_element_type=jnp.float32)
        put(ga_sc, j, g[:, :LANES])
        put(gx_sc, j, g[:, LANES:])

    blocks_per_seg = seg // rows

    def padded_rows(it):
        start = (it // blocks_per_seg) * pitch + (it % blocks_per_seg) * rows
        return pl.ds(pl.multiple_of(start, SUBLANES), rows)

    def local(i, carry):
        h, p = carry
        rs = pl.ds(i, SUBLANES, stride=pitch)
        half_c = prm_sc[0]
        tr = jnp.tanh(ga_sc[:, rs, :] + prm_sc[1])
        ti = jnp.tanh(gx_sc[:, rs, :] + prm_sc[2])
        log_a = half_c + half_c * tr
        a = jnp.exp(log_a)
        half_a = 0.5 * a
        y = (-0.25 - half_a * half_a) * jnp.tanh(log_a)
        half_beta = jnp.where(y > 0.0, y * lax.rsqrt(y), 0.0)
        x = xc_sc[:, rs, :]
        b = half_beta * (x + x * ti)
        h = a * h + b
        p = a * p
        blk = pl.ds(pl.multiple_of(i * SUBLANES, SUBLANES), SUBLANES)
        hl_sc[:, blk, :] = h
        p_sc[:, blk, :] = p
        return h, p

    h_end, p_end = lax.fori_loop(
        0, seg, local,
        (jnp.zeros(vshape, jnp.float32), jnp.ones(vshape, jnp.float32)), unroll=2)

    sub = lax.broadcasted_iota(jnp.int32, vshape, 1)
    a, b = p_end, h_end
    for d in (1, 2, 4):
        keep = sub >= d
        a_sh = pltpu.roll(a, d, 1)
        b_sh = pltpu.roll(b, d, 1)
        b = jnp.where(keep, a * b_sh + b, b)
        a = jnp.where(keep, a * a_sh, a)
    carry_in = h_sc[...]
    seg_end = a * carry_in + b
    h_in = jnp.where(sub == 0, carry_in, pltpu.roll(seg_end, 1, 1))
    h_sc[...] = jnp.broadcast_to(seg_end[:, SUBLANES - 1:SUBLANES, :], vshape)

    def fix(i, c):
        blk = pl.ds(pl.multiple_of(i * SUBLANES, SUBLANES), SUBLANES)
        xc_sc[:, pl.ds(i, SUBLANES, stride=pitch), :] = hl_sc[:, blk, :] + p_sc[:, blk, :] * h_in
        return c

    lax.fori_loop(0, seg, fix, 0)

    def emit(it, c):
        rs = pl.ds(pl.multiple_of(it * rows, rows), rows)
        ps = padded_rows(it)
        h = jnp.concatenate([xc_sc[j, ps, :] for j in range(N_CT)], axis=-1)
        o_ref[rs, :] = (h * zc_ref[rs, :].astype(jnp.float32)).astype(o_ref.dtype)
        return c

    lax.fori_loop(0, ts // rows, emit, 0)


def _rglru(xbz3, cw, cb, wp, ba, bx, lam, *, ts=256, rows=16):
    b, s, _ = xbz3.shape
    const = pl.Buffered(1)
    vec = lambda: pl.BlockSpec((N_CT, 1, LANES), lambda bi, si: (0, 0, 0))
    tiled = lambda: pltpu.VMEM((N_CT, ts, LANES), jnp.float32)
    padded = lambda: pltpu.VMEM(
        (N_CT, SUBLANES * _segment_pitch(ts // SUBLANES), LANES), jnp.float32)
    tile_major = lambda v: v.reshape(N_CT, 1, LANES)
    return pl.pallas_call(
        functools.partial(_rglru_kernel, ts=ts, rows=rows),
        grid=(b, s // ts),
        in_specs=[
            pl.BlockSpec((None, ts, C_WIDTH), lambda bi, si: (bi, si, 0)),
            pl.BlockSpec((None, ts, C_WIDTH), lambda bi, si: (bi, si, 1)),
            pl.BlockSpec((C_CONV, C_WIDTH), lambda bi, si: (0, 0)),
            pl.BlockSpec((1, C_WIDTH), lambda bi, si: (0, 0)),
            pl.BlockSpec((2, C_HEADS, GATE_PAD, GATE_PAD), lambda bi, si: (0, 0, 0, 0),
                         pipeline_mode=const),
            vec(), vec(), vec(),
        ],
        out_specs=pl.BlockSpec((None, ts, C_WIDTH), lambda bi, si: (bi, si, 0)),
        out_shape=jax.ShapeDtypeStruct((b, s, C_WIDTH), jnp.bfloat16),
        scratch_shapes=[
            pltpu.VMEM((ts + SUBLANES, C_WIDTH), jnp.float32),
            padded(),
            pltpu.VMEM((ts, C_WIDTH), jnp.bfloat16),
            padded(),
            padded(),
            pltpu.VMEM((N_CT, SUBLANES, LANES), jnp.float32),
            pltpu.VMEM((N_CT, GATE_WIN, 2 * LANES), jnp.bfloat16),
            pltpu.VMEM((GATE_WIN, 2 * LANES), jnp.float32),
            tiled(),
            tiled(),
            pltpu.VMEM((3, N_CT, SUBLANES, LANES), jnp.float32),
        ],
        compiler_params=pltpu.CompilerParams(
            dimension_semantics=("arbitrary", "arbitrary"),
            vmem_limit_bytes=VMEM_LIMIT),
        name="rglru",
    )(xbz3, xbz3, cw, cb, wp, tile_major(ba), tile_major(bx), tile_major(lam))


def _out1_kernel(hz_ref, x_ref, w_ref, g_ref, o_ref):
    y = jnp.dot(hz_ref[...].astype(w_ref.dtype), w_ref[...],
                preferred_element_type=jnp.float32)
    o_ref[...] = _rms_rows(x_ref[...] + y, g_ref[...])


def _out1(hz, x1, w_bf, g, *, tm=256):
    t = x1.shape[0]
    return pl.pallas_call(
        _out1_kernel,
        grid=(t // tm,),
        in_specs=[
            pl.BlockSpec((tm, C_WIDTH), lambda i: (i, 0)),
            pl.BlockSpec((tm, D_MODEL), lambda i: (i, 0)),
            pl.BlockSpec((C_WIDTH, D_MODEL), lambda i: (0, 0),
                         pipeline_mode=pl.Buffered(1)),
            pl.BlockSpec((1, D_MODEL), lambda i: (0, 0)),
        ],
        out_specs=pl.BlockSpec((tm, D_MODEL), lambda i: (i, 0)),
        out_shape=jax.ShapeDtypeStruct((t, D_MODEL), jnp.float32),
        compiler_params=pltpu.CompilerParams(
            dimension_semantics=("arbitrary",),
            vmem_limit_bytes=VMEM_LIMIT),
        name="out1",
    )(hz, x1, w_bf, g)


def kernel(x, ab_norm, ab_w_in, ab_lambda, ab_head_norm, ab_sgu_ln_g, ab_sgu_ln_b, ab_sgu_w, ab_sgu_b, ab_w_out, c_norm, c_w_in, c_conv_w, c_conv_b, c_gate_a_w, c_gate_a_b, c_gate_x_w, c_gate_x_b, c_lambda, c_w_out, final_norm):
    bsz, seq, d = x.shape
    t = bsz * seq
    bf = jnp.bfloat16
    row = lambda v: v.reshape(1, -1)
    x2 = x.reshape(t, d)

    proj = _proj0(x2, row(ab_norm[0]), ab_w_in[0])
    ya = _attention(proj.reshape(bsz, seq, AB_IN), ab_lambda[0], row(ab_head_norm[0]))
    x1, xn1 = _out0(ya.reshape(t, A_WIDTH), proj, x2, ab_w_out[0].astype(bf), row(c_norm[0]),
                    row(ab_sgu_ln_g[0]), row(ab_sgu_ln_b[0]), ab_sgu_w[0], ab_sgu_b[0].T)

    xbz = _proj1(xn1, c_w_in[0])
    padw = GATE_PAD - C_BLOCK
    wp = jnp.pad(jnp.stack([c_gate_a_w[0], c_gate_x_w[0]]).astype(bf),
                 ((0, 0), (0, 0), (0, padw), (0, padw)))
    hz = _rglru(xbz.reshape(bsz, seq, 2 * C_WIDTH), c_conv_w[0], row(c_conv_b[0]), wp,
                row(c_gate_a_b[0]), row(c_gate_x_b[0]), row(c_lambda[0]))
    out = _out1(hz.reshape(t, C_WIDTH), x1, c_w_out[0], row(final_norm))
    return out.reshape(bsz, seq, d)
```

```python
import functools
import math

import jax
import jax.numpy as jnp
import numpy as np
from jax import lax
from jax.experimental import pallas as pl
from jax.experimental.pallas import tpu as pltpu

D_MODEL = 2048
CHUNK = 64
CHUNK_SHIFT = 6
NORM_EPS = 1e-6
A_HEADS = 8
A_HEAD_DIM = 128
A_VDIM = 2 * A_HEAD_DIM
A_WIDTH = A_HEADS * A_VDIM
B_GROUPS = 8
B_WIDTH = D_MODEL
B_GROUP_DIM = B_WIDTH // B_GROUPS
B_SPAN = 128
C_WIDTH = 2688
C_HEADS = 16
C_BLOCK = C_WIDTH // C_HEADS
C_CONV = 4
C_GATE_C = 8.0
AB_IN = 4 * A_WIDTH + 3 * B_WIDTH
LAM_INIT_0 = 0.8 - 0.6 * math.exp(-0.3 * 0)

LANES = 128
SUBLANES = 8
LOG2E = 1.4426950408889634
VMEM_LIMIT = 56 * 1024 * 1024

GATE_WIN = 512
GATE_PAD = 256
N_CT = C_WIDTH // LANES


def _gate_window_start(j):
    h0 = (LANES * j) // C_BLOCK
    ws = (C_BLOCK * h0) // LANES * LANES
    return min(ws, C_WIDTH - GATE_WIN)


GATE_WS = tuple(_gate_window_start(j) for j in range(N_CT))


def _silu(z):
    return z * (1.0 / (1.0 + jnp.exp(-z)))


def _gelu_tanh(x):
    c = math.sqrt(2.0 / math.pi)
    return 0.5 * x * (1.0 + jnp.tanh(c * (x + 0.044715 * (x * x * x))))


def _rms_rows(xf, g):
    ms = jnp.mean(xf * xf, axis=-1, keepdims=True)
    return xf * lax.rsqrt(ms + NORM_EPS) * g


def _proj0_kernel(x_ref, g_ref, w_ref, o_ref, xn_sc, *, tiles_per_seg):
    j = pl.program_id(1)

    @pl.when(j == 0)
    def _():
        xn_sc[...] = _rms_rows(x_ref[...], g_ref[...]).astype(xn_sc.dtype)

    seg = j // tiles_per_seg

    def proj():
        return jnp.dot(xn_sc[...], w_ref[...], preferred_element_type=jnp.float32)

    @pl.when(seg <= 2)
    def _():
        scale = jnp.where(seg == 0, A_HEAD_DIM ** -0.5 * LOG2E, 1.0).astype(jnp.float32)
        o_ref[...] = (proj() * scale).astype(o_ref.dtype)

    @pl.when((seg == 3) | (seg == 6))
    def _():
        o_ref[...] = _silu(proj()).astype(o_ref.dtype)

    @pl.when((seg == 4) | (seg == 5))
    def _():
        o_ref[...] = _gelu_tanh(proj()).astype(o_ref.dtype)


def _proj0(x2, g, w_bf, *, tm=1024, tn=1024):
    t = x2.shape[0]
    n = w_bf.shape[1]
    return pl.pallas_call(
        functools.partial(_proj0_kernel, tiles_per_seg=A_WIDTH // tn),
        grid=(t // tm, n // tn),
        in_specs=[
            pl.BlockSpec((tm, D_MODEL), lambda i, j: (i, 0)),
            pl.BlockSpec((1, D_MODEL), lambda i, j: (0, 0)),
            pl.BlockSpec((D_MODEL, tn), lambda i, j: (0, j)),
        ],
        out_specs=pl.BlockSpec((tm, tn), lambda i, j: (i, j)),
        out_shape=jax.ShapeDtypeStruct((t, n), jnp.bfloat16),
        scratch_shapes=[pltpu.VMEM((tm, D_MODEL), w_bf.dtype)],
        compiler_params=pltpu.CompilerParams(
            dimension_semantics=("arbitrary", "arbitrary"),
            vmem_limit_bytes=VMEM_LIMIT),
        name="proj0",
    )(x2, g, w_bf)


def _attn_schedule(nq):
    off = [(i, j) for j in range(nq) for i in range(j + 1, nq)]
    diag = [(i, i) for i in range(nq)]
    steps = off + diag
    return len(off), len(diag), steps


def _attn_kernel(it_ref, jt_ref, lamp_ref, q_ref, k_ref, v_ref, za_ref, hn_ref, o_ref,
                 m_sc, l_sc, acc_sc, s0_sc, s1_sc, *, t, n_off, n_diag):
    s_bufs = (s0_sc, s1_sc)
    m_sc[...] = jnp.full(m_sc.shape, -jnp.inf, jnp.float32)
    l_sc[...] = jnp.zeros(l_sc.shape, jnp.float32)
    acc_sc[...] = jnp.zeros(acc_sc.shape, jnp.float32)
    nt = (((1,), (1,)), ((), ()))

    lp = lamp_ref[...]
    d1 = jnp.sum(lp[0:1, :] * lp[1:2, :], axis=1, keepdims=True)
    d2 = jnp.sum(lp[2:3, :] * lp[3:4, :], axis=1, keepdims=True)
    lam = jnp.exp(d1) - jnp.exp(d2) + LAM_INIT_0

    def rows_of(tile):
        return pl.ds(pl.multiple_of(tile * t, t), t)

    def scores(n, s_ref):
        qs, ks = rows_of(it_ref[n]), rows_of(jt_ref[n])
        s_ref[0:t, :] = lax.dot_general(
            q_ref[qs, :A_HEAD_DIM], k_ref[ks, :A_HEAD_DIM], nt,
            preferred_element_type=jnp.float32)
        s_ref[t:2 * t, :] = lax.dot_general(
            q_ref[qs, A_HEAD_DIM:], k_ref[ks, A_HEAD_DIM:], nt,
            preferred_element_type=jnp.float32)

    def softmax_pv(n, s_ref, masked):
        i = it_ref[n]
        s = s_ref[...]
        if masked:
            rq = lax.broadcasted_iota(jnp.int32, (2 * t, 1), 0)
            rq = lax.shift_right_logical(jnp.where(rq >= t, rq - t, rq), CHUNK_SHIFT)
            ck = lax.shift_right_logical(
                lax.broadcasted_iota(jnp.int32, (1, t), 1), CHUNK_SHIFT)
            s = jnp.where(ck <= rq, s, -jnp.inf)
        m_prev = m_sc[i]
        m_new = jnp.maximum(m_prev, jnp.max(s, axis=1, keepdims=True))
        alpha = jnp.exp2(m_prev - m_new)
        p = jnp.exp2(s - pltpu.repeat(m_new, t // LANES, 1))
        l_sc[i] = alpha * l_sc[i] + jnp.sum(p, axis=1, keepdims=True)
        m_sc[i] = m_new
        pv = jnp.dot(p.astype(jnp.bfloat16), v_ref[rows_of(jt_ref[n]), :],
                     preferred_element_type=jnp.float32)
        acc_sc[i] = acc_sc[i] * pltpu.repeat(alpha, A_VDIM // LANES, 1) + pv

    def finish(i):
        inv_l = 1.0 / l_sc[i]
        acc = acc_sc[i] * pltpu.repeat(inv_l, A_VDIM // LANES, 1)
        o = acc[:t, :] - lam * acc[t:, :]
        ms = jnp.mean(o * o, axis=-1, keepdims=True)
        o = o * lax.rsqrt(ms + NORM_EPS) * hn_ref[...] * (1.0 - LAM_INIT_0)
        rows = rows_of(i)
        o_ref[rows, :] = (o * za_ref[rows, :].astype(jnp.float32)).astype(o_ref.dtype)

    def tick(n, par, *, masked, do_scores=True):
        if do_scores:
            scores(n + 1, s_bufs[1 - par])
        softmax_pv(n, s_bufs[par], masked)
        if masked:
            finish(it_ref[n])

    scores(0, s0_sc)

    def visible_pair(pp, c):
        tick(2 * pp, 0, masked=False)
        tick(2 * pp + 1, 1, masked=False)
        return c

    lax.fori_loop(0, n_off // 2, visible_pair, 0)

    def diagonal_pair(pp, c):
        n = n_off + 2 * pp
        tick(n, 0, masked=True)
        tick(n + 1, 1, masked=True)
        return c

    lax.fori_loop(0, n_diag // 2 - 1, diagonal_pair, 0)
    last = n_off + n_diag - 1
    tick(last - 1, 0, masked=True)
    tick(last, 1, masked=True, do_scores=False)


def _attention(proj3, lamp, hn, *, t=512):
    b, s, _ = proj3.shape
    nblk = A_WIDTH // A_VDIM
    nq = s // t
    n_off, n_diag, steps = _attn_schedule(nq)
    assert n_off % 2 == 0 and n_off >= 2 and n_diag % 2 == 0 and n_diag >= 2
    it = jnp.asarray([st[0] for st in steps], jnp.int32)
    jt = jnp.asarray([st[1] for st in steps], jnp.int32)
    head = lambda blk: pl.BlockSpec((None, s, A_VDIM),
                                    lambda bi, h, it_r, jt_r: (bi, 0, blk * nblk + h))
    grid_spec = pltpu.PrefetchScalarGridSpec(
        num_scalar_prefetch=2,
        grid=(b, A_HEADS),
        in_specs=[
            pl.BlockSpec((4, A_HEAD_DIM), lambda bi, h, it_r, jt_r: (0, 0)),
            head(0), head(1), head(2), head(3),
            pl.BlockSpec((1, A_VDIM), lambda bi, h, it_r, jt_r: (0, h)),
        ],
        out_specs=pl.BlockSpec((None, s, A_VDIM), lambda bi, h, it_r, jt_r: (bi, 0, h)),
        scratch_shapes=[
            pltpu.VMEM((nq, 2 * t, LANES), jnp.float32),
            pltpu.VMEM((nq, 2 * t, LANES), jnp.float32),
            pltpu.VMEM((nq, 2 * t, A_VDIM), jnp.float32),
            pltpu.VMEM((2 * t, t), jnp.float32),
            pltpu.VMEM((2 * t, t), jnp.float32),
        ],
    )
    return pl.pallas_call(
        functools.partial(_attn_kernel, t=t, n_off=n_off, n_diag=n_diag),
        grid_spec=grid_spec,
        out_shape=jax.ShapeDtypeStruct((b, s, A_WIDTH), jnp.bfloat16),
        compiler_params=pltpu.CompilerParams(
            dimension_semantics=("arbitrary", "arbitrary"),
            vmem_limit_bytes=VMEM_LIMIT),
        name="diff_attn",
    )(it, jt, lamp, proj3, proj3, proj3, proj3, hn)


def _sgu_tile(gu_ref, gv_ref, zb_ref, lng_ref, lnb_ref, ws_ref, bst_ref, yb_sc, rows):
    v = gv_ref[...].astype(jnp.float32)
    mu = jnp.mean(v, axis=-1, keepdims=True)
    vc = v - mu
    var = jnp.mean(vc * vc, axis=-1, keepdims=True)
    vln = (vc * lax.rsqrt(var + NORM_EPS) * lng_ref[...] + lnb_ref[...])
    vln = vln.astype(jnp.bfloat16)
    ci = lax.broadcasted_iota(jnp.int32, (B_SPAN, B_SPAN), 0) // CHUNK
    cj = lax.broadcasted_iota(jnp.int32, (B_SPAN, B_SPAN), 1) // CHUNK
    keep = ci >= cj
    for g in range(B_GROUPS):
        w = jnp.where(keep, ws_ref[g], 0.0).astype(jnp.bfloat16)
        bcol = bst_ref[:, g:g + 1]
        c0 = g * B_GROUP_DIM
        for sp in range(rows // B_SPAN):
            r0 = sp * B_SPAN
            mixed = jnp.dot(w, vln[r0:r0 + B_SPAN, c0:c0 + B_GROUP_DIM],
                            preferred_element_type=jnp.float32) + bcol
            u = gu_ref[r0:r0 + B_SPAN, c0:c0 + B_GROUP_DIM].astype(jnp.float32)
            z = zb_ref[r0:r0 + B_SPAN, c0:c0 + B_GROUP_DIM].astype(jnp.float32)
            yb_sc[r0:r0 + B_SPAN, c0:c0 + B_GROUP_DIM] = (u * mixed * z).astype(yb_sc.dtype)


def _out0_kernel(ya_ref, gu_ref, gv_ref, zb_ref, x_ref, wa_ref, wb_ref, g_ref,
                 lng_ref, lnb_ref, ws_ref, bst_ref, x1_ref, xn_ref, yb_sc, *, rows):
    y = jnp.dot(ya_ref[...], wa_ref[...], preferred_element_type=jnp.float32)
    _sgu_tile(gu_ref, gv_ref, zb_ref, lng_ref, lnb_ref, ws_ref, bst_ref, yb_sc, rows)
    y = y + jnp.dot(yb_sc[...], wb_ref[...], preferred_element_type=jnp.float32)
    x1 = x_ref[...] + y
    x1_ref[...] = x1
    xn_ref[...] = _rms_rows(x1, g_ref[...]).astype(xn_ref.dtype)


def _out0(ya, proj, x2, w_bf, g, ln_g, ln_b, w_s, b_st, *, tm=256):
    t = x2.shape[0]
    const = pl.Buffered(1)
    base = 4 * A_WIDTH // B_WIDTH
    vec = lambda n: pl.BlockSpec((1, n), lambda i: (0, 0))
    return pl.pallas_call(
        functools.partial(_out0_kernel, rows=tm),
        grid=(t // tm,),
        in_specs=[
            pl.BlockSpec((tm, A_WIDTH), lambda i: (i, 0)),
            pl.BlockSpec((tm, B_WIDTH), lambda i: (i, base)),
            pl.BlockSpec((tm, B_WIDTH), lambda i: (i, base + 1)),
            pl.BlockSpec((tm, B_WIDTH), lambda i: (i, base + 2)),
            pl.BlockSpec((tm, D_MODEL), lambda i: (i, 0)),
            pl.BlockSpec((A_WIDTH, D_MODEL), lambda i: (0, 0), pipeline_mode=const),
            pl.BlockSpec((B_WIDTH, D_MODEL), lambda i: (1, 0), pipeline_mode=const),
            vec(D_MODEL), vec(B_WIDTH), vec(B_WIDTH),
            pl.BlockSpec((B_GROUPS, B_SPAN, B_SPAN), lambda i: (0, 0, 0)),
            pl.BlockSpec((B_SPAN, B_GROUPS), lambda i: (0, 0)),
        ],
        out_specs=[
            pl.BlockSpec((tm, D_MODEL), lambda i: (i, 0)),
            pl.BlockSpec((tm, D_MODEL), lambda i: (i, 0)),
        ],
        out_shape=[
            jax.ShapeDtypeStruct((t, D_MODEL), jnp.float32),
            jax.ShapeDtypeStruct((t, D_MODEL), jnp.bfloat16),
        ],
        scratch_shapes=[pltpu.VMEM((tm, B_WIDTH), jnp.bfloat16)],
        compiler_params=pltpu.CompilerParams(
            dimension_semantics=("arbitrary",),
            vmem_limit_bytes=VMEM_LIMIT),
        name="out0",
    )(ya, proj, proj, proj, x2, w_bf, w_bf, g, ln_g, ln_b, w_s, b_st)


def _proj1_kernel(xn_ref, w_ref, o_ref):
    def proj():
        return jnp.dot(xn_ref[...].astype(w_ref.dtype), w_ref[...],
                       preferred_element_type=jnp.float32)

    @pl.when(pl.program_id(0) == 0)
    def _():
        o_ref[...] = proj().astype(o_ref.dtype)

    @pl.when(pl.program_id(0) == 1)
    def _():
        o_ref[...] = _silu(proj()).astype(o_ref.dtype)


def _proj1(xn, w_bf, *, tm=512):
    t = xn.shape[0]
    return pl.pallas_call(
        _proj1_kernel,
        grid=(2, t // tm),
        in_specs=[
            pl.BlockSpec((tm, D_MODEL), lambda j, i: (i, 0)),
            pl.BlockSpec((D_MODEL, C_WIDTH), lambda j, i: (0, j),
                         pipeline_mode=pl.Buffered(1)),
        ],
        out_specs=pl.BlockSpec((tm, C_WIDTH), lambda j, i: (i, j)),
        out_shape=jax.ShapeDtypeStruct((t, 2 * C_WIDTH), jnp.bfloat16),
        compiler_params=pltpu.CompilerParams(
            dimension_semantics=("arbitrary", "arbitrary"),
            vmem_limit_bytes=VMEM_LIMIT),
        name="proj1",
    )(xn, w_bf)


def _segment_pitch(seg):
    units = seg // SUBLANES
    return SUBLANES * (units + 1 if units % 2 == 0 else units)


def _build_gate_slabs(wp_ref, wg_sc, tmp_sc):
    row = lax.broadcasted_iota(jnp.int32, (GATE_PAD, LANES), 0)
    col = lax.broadcasted_iota(jnp.int32, (GATE_PAD, LANES), 1)
    for j in range(N_CT):
        tmp_sc[...] = jnp.zeros(tmp_sc.shape, jnp.float32)
        h_lo = (LANES * j) // C_BLOCK
        h_hi = (LANES * j + LANES - 1) // C_BLOCK
        for gate in range(2):
            for h in range(h_lo, h_hi + 1):
                sel = jnp.where(row == col + (LANES * j - C_BLOCK * h), 1.0, 0.0)
                part = jnp.dot(wp_ref[gate, h], sel.astype(jnp.bfloat16),
                               preferred_element_type=jnp.float32)
                r0 = C_BLOCK * h - GATE_WS[j]
                cs = slice(gate * LANES, (gate + 1) * LANES)
                tmp_sc[r0:r0 + C_BLOCK, cs] = tmp_sc[r0:r0 + C_BLOCK, cs] + part[:C_BLOCK, :]
        wg_sc[j] = (0.5 * tmp_sc[...]).astype(jnp.bfloat16)


def _rglru_kernel(xb_ref, zc_ref, cw_ref, cb_ref, wp_ref, ba_ref, bx_ref,
                  lam_ref, o_ref, pad_sc, xc_sc, xcb_sc, ga_sc, gx_sc, h_sc,
                  wg_ref, tmp_sc, hl_sc, p_sc, prm_sc, *, ts, rows):
    si = pl.program_id(1)
    halo = SUBLANES
    vshape = (N_CT, SUBLANES, LANES)

    @pl.when((pl.program_id(0) == 0) & (si == 0))
    def _():
        _build_gate_slabs(wp_ref, wg_ref, tmp_sc)
        lam = lam_ref[...]
        sp = jnp.maximum(-lam, 0.0) + jnp.log1p(jnp.exp(-jnp.abs(lam)))
        prm_sc[0] = jnp.broadcast_to((-0.5 * C_GATE_C) * sp, vshape)
        prm_sc[1] = jnp.broadcast_to(0.5 * ba_ref[...], vshape)
        prm_sc[2] = jnp.broadcast_to(0.5 * bx_ref[...], vshape)

    @pl.when(si == 0)
    def _():
        pad_sc[0:halo, :] = jnp.zeros((halo, C_WIDTH), jnp.float32)
        h_sc[...] = jnp.zeros(h_sc.shape, jnp.float32)

    @pl.when(si > 0)
    def _():
        pad_sc[0:halo, :] = pad_sc[ts:ts + halo, :]

    pad_sc[halo:halo + ts, :] = xb_ref[...].astype(jnp.float32)
    xc = cb_ref[...] + cw_ref[C_CONV - 1:C_CONV, :] * pad_sc[halo:halo + ts, :]
    for tap in range(C_CONV - 1):
        off = halo - (C_CONV - 1) + tap
        xc = xc + cw_ref[tap:tap + 1, :] * pad_sc[off:off + ts, :]
    xcb_sc[...] = xc.astype(jnp.bfloat16)
    seg = ts // SUBLANES
    pitch = _segment_pitch(seg)

    def put(dst_sc, j, val):
        for s in range(SUBLANES):
            dst_sc[j, s * pitch:s * pitch + seg, :] = val[s * seg:(s + 1) * seg, :]

    for j in range(N_CT):
        put(xc_sc, j, xc[:, j * LANES:(j + 1) * LANES])

    for j in range(N_CT):
        ws = GATE_WS[j]
        g = jnp.dot(xcb_sc[:, ws:ws + GATE_WIN], wg_ref[j],
                    preferred_element_type=jnp.float32)
        put(ga_sc, j, g[:, :LANES])
        put(gx_sc, j, g[:, LANES:])

    blocks_per_seg = seg // rows

    def padded_rows(it):
        start = (it // blocks_per_seg) * pitch + (it % blocks_per_seg) * rows
        return pl.ds(pl.multiple_of(start, SUBLANES), rows)

    def local(i, carry):
        h, p = carry
        rs = pl.ds(i, SUBLANES, stride=pitch)
        half_c = prm_sc[0]
        tr = jnp.tanh(ga_sc[:, rs, :] + prm_sc[1])
        ti = jnp.tanh(gx_sc[:, rs, :] + prm_sc[2])
        log_a = half_c + half_c * tr
        a = jnp.exp(log_a)
        half_a = 0.5 * a
        y = (-0.25 - half_a * half_a) * jnp.tanh(log_a)
        half_beta = jnp.where(y > 0.0, y * lax.rsqrt(y), 0.0)
        x = xc_sc[:, rs, :]
        b = half_beta * (x + x * ti)
        h = a * h + b
        p = a * p
        blk = pl.ds(pl.multiple_of(i * SUBLANES, SUBLANES), SUBLANES)
        hl_sc[:, blk, :] = h
        p_sc[:, blk, :] = p
        return h, p

    h_end, p_end = lax.fori_loop(
        0, seg, local,
        (jnp.zeros(vshape, jnp.float32), jnp.ones(vshape, jnp.float32)), unroll=2)

    sub = lax.broadcasted_iota(jnp.int32, vshape, 1)
    a, b = p_end, h_end
    for d in (1, 2, 4):
        keep = sub >= d
        a_sh = pltpu.roll(a, d, 1)
        b_sh = pltpu.roll(b, d, 1)
        b = jnp.where(keep, a * b_sh + b, b)
        a = jnp.where(keep, a * a_sh, a)
    carry_in = h_sc[...]
    seg_end = a * carry_in + b
    h_in = jnp.where(sub == 0, carry_in, pltpu.roll(seg_end, 1, 1))
    h_sc[...] = jnp.broadcast_to(seg_end[:, SUBLANES - 1:SUBLANES, :], vshape)

    def fix(i, c):
        blk = pl.ds(pl.multiple_of(i * SUBLANES, SUBLANES), SUBLANES)
        xc_sc[:, pl.ds(i, SUBLANES, stride=pitch), :] = hl_sc[:, blk, :] + p_sc[:, blk, :] * h_in
        return c

    lax.fori_loop(0, seg, fix, 0)

    def emit(it, c):
        rs = pl.ds(pl.multiple_of(it * rows, rows), rows)
        ps = padded_rows(it)
        h = jnp.concatenate([xc_sc[j, ps, :] for j in range(N_CT)], axis=-1)
        o_ref[rs, :] = (h * zc_ref[rs, :].astype(jnp.float32)).astype(o_ref.dtype)
        return c

    lax.fori_loop(0, ts // rows, emit, 0)


def _rglru(xbz3, cw, cb, wp, ba, bx, lam, *, ts=256, rows=16):
    b, s, _ = xbz3.shape
    const = pl.Buffered(1)
    vec = lambda: pl.BlockSpec((N_CT, 1, LANES), lambda bi, si: (0, 0, 0))
    tiled = lambda: pltpu.VMEM((N_CT, ts, LANES), jnp.float32)
    padded = lambda: pltpu.VMEM(
        (N_CT, SUBLANES * _segment_pitch(ts // SUBLANES), LANES), jnp.float32)
    tile_major = lambda v: v.reshape(N_CT, 1, LANES)
    return pl.pallas_call(
        functools.partial(_rglru_kernel, ts=ts, rows=rows),
        grid=(b, s // ts),
        in_specs=[
            pl.BlockSpec((None, ts, C_WIDTH), lambda bi, si: (bi, si, 0)),
            pl.BlockSpec((None, ts, C_WIDTH), lambda bi, si: (bi, si, 1)),
            pl.BlockSpec((C_CONV, C_WIDTH), lambda bi, si: (0, 0)),
            pl.BlockSpec((1, C_WIDTH), lambda bi, si: (0, 0)),
            pl.BlockSpec((2, C_HEADS, GATE_PAD, GATE_PAD), lambda bi, si: (0, 0, 0, 0),
                         pipeline_mode=const),
            vec(), vec(), vec(),
        ],
        out_specs=pl.BlockSpec((None, ts, C_WIDTH), lambda bi, si: (bi, si, 0)),
        out_shape=jax.ShapeDtypeStruct((b, s, C_WIDTH), jnp.bfloat16),
        scratch_shapes=[
            pltpu.VMEM((ts + SUBLANES, C_WIDTH), jnp.float32),
            padded(),
            pltpu.VMEM((ts, C_WIDTH), jnp.bfloat16),
            padded(),
            padded(),
            pltpu.VMEM((N_CT, SUBLANES, LANES), jnp.float32),
            pltpu.VMEM((N_CT, GATE_WIN, 2 * LANES), jnp.bfloat16),
            pltpu.VMEM((GATE_WIN, 2 * LANES), jnp.float32),
            tiled(),
            tiled(),
            pltpu.VMEM((3, N_CT, SUBLANES, LANES), jnp.float32),
        ],
        compiler_params=pltpu.CompilerParams(
            dimension_semantics=("arbitrary", "arbitrary"),
            vmem_limit_bytes=VMEM_LIMIT),
        name="rglru",
    )(xbz3, xbz3, cw, cb, wp, tile_major(ba), tile_major(bx), tile_major(lam))


def _out1_kernel(hz_ref, x_ref, w_ref, g_ref, o_ref):
    y = jnp.dot(hz_ref[...].astype(w_ref.dtype), w_ref[...],
                preferred_element_type=jnp.float32)
    o_ref[...] = _rms_rows(x_ref[...] + y, g_ref[...])


def _out1(hz, x1, w_bf, g, *, tm=256):
    t = x1.shape[0]
    return pl.pallas_call(
        _out1_kernel,
        grid=(t // tm,),
        in_specs=[
            pl.BlockSpec((tm, C_WIDTH), lambda i: (i, 0)),
            pl.BlockSpec((tm, D_MODEL), lambda i: (i, 0)),
            pl.BlockSpec((C_WIDTH, D_MODEL), lambda i: (0, 0),
                         pipeline_mode=pl.Buffered(1)),
            pl.BlockSpec((1, D_MODEL), lambda i: (0, 0)),
        ],
        out_specs=pl.BlockSpec((tm, D_MODEL), lambda i: (i, 0)),
        out_shape=jax.ShapeDtypeStruct((t, D_MODEL), jnp.float32),
        compiler_params=pltpu.CompilerParams(
            dimension_semantics=("arbitrary",),
            vmem_limit_bytes=VMEM_LIMIT),
        name="out1",
    )(hz, x1, w_bf, g)


def kernel(x, ab_norm, ab_w_in, ab_lambda, ab_head_norm, ab_sgu_ln_g, ab_sgu_ln_b, ab_sgu_w, ab_sgu_b, ab_w_out, c_norm, c_w_in, c_conv_w, c_conv_b, c_gate_a_w, c_gate_a_b, c_gate_x_w, c_gate_x_b, c_lambda, c_w_out, final_norm):
    bsz, seq, d = x.shape
    t = bsz * seq
    bf = jnp.bfloat16
    row = lambda v: v.reshape(1, -1)
    x2 = x.reshape(t, d)

    proj = _proj0(x2, row(ab_norm[0]), ab_w_in[0])
    ya = _attention(proj.reshape(bsz, seq, AB_IN), ab_lambda[0], row(ab_head_norm[0]))
    x1, xn1 = _out0(ya.reshape(t, A_WIDTH), proj, x2, ab_w_out[0].astype(bf), row(c_norm[0]),
                    row(ab_sgu_ln_g[0]), row(ab_sgu_ln_b[0]), ab_sgu_w[0], ab_sgu_b[0].T)

    xbz = _proj1(xn1, c_w_in[0])
    padw = GATE_PAD - C_BLOCK
    wp = jnp.pad(jnp.stack([c_gate_a_w[0], c_gate_x_w[0]]).astype(bf),
                 ((0, 0), (0, 0), (0, padw), (0, padw)))
    hz = _rglru(xbz.reshape(bsz, seq, 2 * C_WIDTH), c_conv_w[0], row(c_conv_b[0]), wp,
                row(c_gate_a_b[0]), row(c_gate_x_b[0]), row(c_lambda[0]))
    out = _out1(hz.reshape(t, C_WIDTH), x1, c_w_out[0], row(final_norm))
    return out.reshape(bsz, seq, d)
```

```python
import functools
import math

import jax
import jax.numpy as jnp
import numpy as np
from jax import lax
from jax.experimental import pallas as pl
from jax.experimental.pallas import tpu as pltpu

D_MODEL = 2048
CHUNK = 64
CHUNK_SHIFT = 6
NORM_EPS = 1e-6
A_HEADS = 8
A_HEAD_DIM = 128
A_VDIM = 2 * A_HEAD_DIM
A_WIDTH = A_HEADS * A_VDIM
B_GROUPS = 8
B_WIDTH = D_MODEL
B_GROUP_DIM = B_WIDTH // B_GROUPS
B_SPAN = 128
C_WIDTH = 2688
C_HEADS = 16
C_BLOCK = C_WIDTH // C_HEADS
C_CONV = 4
C_GATE_C = 8.0
AB_IN = 4 * A_WIDTH + 3 * B_WIDTH
LAM_INIT_0 = 0.8 - 0.6 * math.exp(-0.3 * 0)

LANES = 128
SUBLANES = 8
LOG2E = 1.4426950408889634
VMEM_LIMIT = 56 * 1024 * 1024

GATE_WIN = 512
GATE_PAD = 256
N_CT = C_WIDTH // LANES


def _gate_window_start(j):
    h0 = (LANES * j) // C_BLOCK
    ws = (C_BLOCK * h0) // LANES * LANES
    return min(ws, C_WIDTH - GATE_WIN)


GATE_WS = tuple(_gate_window_start(j) for j in range(N_CT))


def _silu(z):
    h = 0.5 * z
    return h + h * jnp.tanh(h)


def _gelu_tanh(x):
    c = math.sqrt(2.0 / math.pi)
    h = 0.5 * x
    return h + h * jnp.tanh(x * (c + (c * 0.044715) * (x * x)))


def _rms_rows(xf, g):
    ms = jnp.mean(xf * xf, axis=-1, keepdims=True)
    return xf * lax.rsqrt(ms + NORM_EPS) * g


def _proj0_kernel(x_ref, g_ref, w_ref, o_ref, xn_sc, *, tiles_per_seg):
    j = pl.program_id(1)

    @pl.when(j == 0)
    def _():
        xn_sc[...] = _rms_rows(x_ref[...], g_ref[...]).astype(xn_sc.dtype)

    seg = j // tiles_per_seg

    def proj():
        return jnp.dot(xn_sc[...], w_ref[...], preferred_element_type=jnp.float32)

    @pl.when(seg <= 2)
    def _():
        scale = jnp.where(seg == 0, A_HEAD_DIM ** -0.5 * LOG2E, 1.0).astype(jnp.float32)
        o_ref[...] = (proj() * scale).astype(o_ref.dtype)

    @pl.when((seg == 3) | (seg == 6))
    def _():
        o_ref[...] = _silu(proj()).astype(o_ref.dtype)

    @pl.when((seg == 4) | (seg == 5))
    def _():
        o_ref[...] = _gelu_tanh(proj()).astype(o_ref.dtype)


def _proj0(x2, g, w_bf, *, tm=1024, tn=1024):
    t = x2.shape[0]
    n = w_bf.shape[1]
    return pl.pallas_call(
        functools.partial(_proj0_kernel, tiles_per_seg=A_WIDTH // tn),
        grid=(t // tm, n // tn),
        in_specs=[
            pl.BlockSpec((tm, D_MODEL), lambda i, j: (i, 0)),
            pl.BlockSpec((1, D_MODEL), lambda i, j: (0, 0)),
            pl.BlockSpec((D_MODEL, tn), lambda i, j: (0, j)),
        ],
        out_specs=pl.BlockSpec((tm, tn), lambda i, j: (i, j)),
        out_shape=jax.ShapeDtypeStruct((t, n), jnp.bfloat16),
        scratch_shapes=[pltpu.VMEM((tm, D_MODEL), w_bf.dtype)],
        compiler_params=pltpu.CompilerParams(
            dimension_semantics=("arbitrary", "arbitrary"),
            vmem_limit_bytes=VMEM_LIMIT),
        name="proj0",
    )(x2, g, w_bf)


def _attn_schedule(nq):
    off = [(i, j) for j in range(nq) for i in range(j + 1, nq)]
    diag = [(i, i) for i in range(nq)]
    steps = off + diag
    n_off, last = len(off), len(steps) - 1
    first_visit = {}
    for n, (i, _) in enumerate(steps):
        first_visit.setdefault(i, n)
    flags = [(n >= n_off, first_visit[i] == n, n < last) for n, (i, _) in enumerate(steps)]
    assert len(steps) % 2 == 0
    pairs = [(flags[n], flags[n + 1]) for n in range(0, len(steps), 2)]
    runs = []
    for k, pf in enumerate(pairs):
        if runs and runs[-1][2] == pf:
            runs[-1] = (runs[-1][0], runs[-1][1] + 1, pf)
        else:
            runs.append((k, 1, pf))
    return steps, tuple(runs)


def _attn_kernel(it_ref, jt_ref, lamp_ref, q_ref, k_ref, v_ref, za_ref, hn_ref, o_ref,
                 m_sc, l_sc, acc_sc, s0_sc, s1_sc, *, t, pair_runs):
    s_bufs = (s0_sc, s1_sc)
    nt = (((1,), (1,)), ((), ()))

    lp = lamp_ref[...]
    d1 = jnp.sum(lp[0:1, :] * lp[1:2, :], axis=1, keepdims=True)
    d2 = jnp.sum(lp[2:3, :] * lp[3:4, :], axis=1, keepdims=True)
    lam = jnp.exp(d1) - jnp.exp(d2) + LAM_INIT_0

    def rows_of(tile):
        return pl.ds(pl.multiple_of(tile * t, t), t)

    def scores(n, s_ref):
        qs, ks = rows_of(it_ref[n]), rows_of(jt_ref[n])
        s_ref[0:t, :] = lax.dot_general(
            q_ref[qs, :A_HEAD_DIM], k_ref[ks, :A_HEAD_DIM], nt,
            preferred_element_type=jnp.float32)
        s_ref[t:2 * t, :] = lax.dot_general(
            q_ref[qs, A_HEAD_DIM:], k_ref[ks, A_HEAD_DIM:], nt,
            preferred_element_type=jnp.float32)

    def softmax_pv(n, s_ref, masked, first):
        i = it_ref[n]
        s = s_ref[...]
        if masked:
            rq = lax.broadcasted_iota(jnp.int32, (2 * t, 1), 0)
            rq = lax.shift_right_logical(jnp.where(rq >= t, rq - t, rq), CHUNK_SHIFT)
            ck = lax.shift_right_logical(
                lax.broadcasted_iota(jnp.int32, (1, t), 1), CHUNK_SHIFT)
            s = jnp.where(ck <= rq, s, -jnp.inf)
        v = v_ref[rows_of(jt_ref[n]), :]
        if first:
            m_new = jnp.broadcast_to(jnp.max(s, axis=1, keepdims=True), (2 * t, LANES))
            p = jnp.exp2(s - pltpu.repeat(m_new, t // LANES, 1))
            l_sc[i] = jnp.broadcast_to(jnp.sum(p, axis=1, keepdims=True), (2 * t, LANES))
            m_sc[i] = m_new
            acc_sc[i] = jnp.dot(p.astype(jnp.bfloat16), v, preferred_element_type=jnp.float32)
            return
        m_prev = m_sc[i]
        m_new = jnp.maximum(m_prev, jnp.max(s, axis=1, keepdims=True))
        alpha = jnp.exp2(m_prev - m_new)
        p = jnp.exp2(s - pltpu.repeat(m_new, t // LANES, 1))
        l_sc[i] = alpha * l_sc[i] + jnp.sum(p, axis=1, keepdims=True)
        m_sc[i] = m_new
        pv = jnp.dot(p.astype(jnp.bfloat16), v,
                     preferred_element_type=jnp.float32)
        acc_sc[i] = acc_sc[i] * pltpu.repeat(alpha, A_VDIM // LANES, 1) + pv

    def finish(i):
        inv_l = 1.0 / l_sc[i]
        acc = acc_sc[i] * pltpu.repeat(inv_l, A_VDIM // LANES, 1)
        o = acc[:t, :] - lam * acc[t:, :]
        ms = jnp.mean(o * o, axis=-1, keepdims=True)
        o = o * lax.rsqrt(ms + NORM_EPS) * hn_ref[...] * (1.0 - LAM_INIT_0)
        rows = rows_of(i)
        o_ref[rows, :] = (o * za_ref[rows, :].astype(jnp.float32)).astype(o_ref.dtype)

    def tick(n, par, masked, first, do_scores):
        if do_scores:
            scores(n + 1, s_bufs[1 - par])
        softmax_pv(n, s_bufs[par], masked, first)
        if masked:
            finish(it_ref[n])

    scores(0, s0_sc)
    for start, count, flags in pair_runs:
        def pair(pp, c, start=start, flags=flags):
            n = 2 * (start + pp)
            tick(n, 0, *flags[0])
            tick(n + 1, 1, *flags[1])
            return c
        if count == 1:
            pair(0, 0)
        else:
            lax.fori_loop(0, count, pair, 0)


def _attention(proj3, lamp, hn, *, t=512):
    b, s, _ = proj3.shape
    nblk = A_WIDTH // A_VDIM
    nq = s // t
    steps, pair_runs = _attn_schedule(nq)
    it = jnp.asarray([st[0] for st in steps], jnp.int32)
    jt = jnp.asarray([st[1] for st in steps], jnp.int32)
    head = lambda blk: pl.BlockSpec((None, s, A_VDIM),
                                    lambda bi, h, it_r, jt_r: (bi, 0, blk * nblk + h))
    grid_spec = pltpu.PrefetchScalarGridSpec(
        num_scalar_prefetch=2,
        grid=(b, A_HEADS),
        in_specs=[
            pl.BlockSpec((4, A_HEAD_DIM), lambda bi, h, it_r, jt_r: (0, 0)),
            head(0), head(1), head(2), head(3),
            pl.BlockSpec((1, A_VDIM), lambda bi, h, it_r, jt_r: (0, h)),
        ],
        out_specs=pl.BlockSpec((None, s, A_VDIM), lambda bi, h, it_r, jt_r: (bi, 0, h)),
        scratch_shapes=[
            pltpu.VMEM((nq, 2 * t, LANES), jnp.float32),
            pltpu.VMEM((nq, 2 * t, LANES), jnp.float32),
            pltpu.VMEM((nq, 2 * t, A_VDIM), jnp.float32),
            pltpu.VMEM((2 * t, t), jnp.float32),
            pltpu.VMEM((2 * t, t), jnp.float32),
        ],
    )
    return pl.pallas_call(
        functools.partial(_attn_kernel, t=t, pair_runs=pair_runs),
        grid_spec=grid_spec,
        out_shape=jax.ShapeDtypeStruct((b, s, A_WIDTH), jnp.bfloat16),
        compiler_params=pltpu.CompilerParams(
            dimension_semantics=("arbitrary", "arbitrary"),
            vmem_limit_bytes=VMEM_LIMIT),
        name="diff_attn",
    )(it, jt, lamp, proj3, proj3, proj3, proj3, hn)


def _sgu_tile(gu_ref, gv_ref, zb_ref, lng_ref, lnb_ref, ws_ref, bst_ref, yb_sc, rows):
    v = gv_ref[...].astype(jnp.float32)
    mu = jnp.mean(v, axis=-1, keepdims=True)
    vc = v - mu
    var = jnp.mean(vc * vc, axis=-1, keepdims=True)
    vln = (vc * lax.rsqrt(var + NORM_EPS) * lng_ref[...] + lnb_ref[...])
    vln = vln.astype(jnp.bfloat16)
    ci = lax.broadcasted_iota(jnp.int32, (B_SPAN, B_SPAN), 0) // CHUNK
    cj = lax.broadcasted_iota(jnp.int32, (B_SPAN, B_SPAN), 1) // CHUNK
    keep = ci >= cj
    for g in range(B_GROUPS):
        w = jnp.where(keep, ws_ref[g], 0.0).astype(jnp.bfloat16)
        bcol = bst_ref[:, g:g + 1]
        c0 = g * B_GROUP_DIM
        for sp in range(rows // B_SPAN):
            r0 = sp * B_SPAN
            mixed = jnp.dot(w, vln[r0:r0 + B_SPAN, c0:c0 + B_GROUP_DIM],
                            preferred_element_type=jnp.float32) + bcol
            u = gu_ref[r0:r0 + B_SPAN, c0:c0 + B_GROUP_DIM].astype(jnp.float32)
            z = zb_ref[r0:r0 + B_SPAN, c0:c0 + B_GROUP_DIM].astype(jnp.float32)
            yb_sc[r0:r0 + B_SPAN, c0:c0 + B_GROUP_DIM] = (u * mixed * z).astype(yb_sc.dtype)


def _out0_kernel(ya_ref, gu_ref, gv_ref, zb_ref, x_ref, wa_ref, wb_ref, g_ref,
                 lng_ref, lnb_ref, ws_ref, bst_ref, x1_ref, xn_ref, yb_sc, *, rows):
    y = jnp.dot(ya_ref[...], wa_ref[...], preferred_element_type=jnp.float32)
    _sgu_tile(gu_ref, gv_ref, zb_ref, lng_ref, lnb_ref, ws_ref, bst_ref, yb_sc, rows)
    y = y + jnp.dot(yb_sc[...], wb_ref[...], preferred_element_type=jnp.float32)
    x1 = x_ref[...] + y
    x1_ref[...] = x1
    xn_ref[...] = _rms_rows(x1, g_ref[...]).astype(xn_ref.dtype)


def _out0(ya, proj, x2, w_bf, g, ln_g, ln_b, w_s, b_st, *, tm=256):
    t = x2.shape[0]
    const = pl.Buffered(1)
    base = 4 * A_WIDTH // B_WIDTH
    vec = lambda n: pl.BlockSpec((1, n), lambda i: (0, 0))
    return pl.pallas_call(
        functools.partial(_out0_kernel, rows=tm),
        grid=(t // tm,),
        in_specs=[
            pl.BlockSpec((tm, A_WIDTH), lambda i: (i, 0)),
            pl.BlockSpec((tm, B_WIDTH), lambda i: (i, base)),
            pl.BlockSpec((tm, B_WIDTH), lambda i: (i, base + 1)),
            pl.BlockSpec((tm, B_WIDTH), lambda i: (i, base + 2)),
            pl.BlockSpec((tm, D_MODEL), lambda i: (i, 0)),
            pl.BlockSpec((A_WIDTH, D_MODEL), lambda i: (0, 0), pipeline_mode=const),
            pl.BlockSpec((B_WIDTH, D_MODEL), lambda i: (1, 0), pipeline_mode=const),
            vec(D_MODEL), vec(B_WIDTH), vec(B_WIDTH),
            pl.BlockSpec((B_GROUPS, B_SPAN, B_SPAN), lambda i: (0, 0, 0)),
            pl.BlockSpec((B_SPAN, B_GROUPS), lambda i: (0, 0)),
        ],
        out_specs=[
            pl.BlockSpec((tm, D_MODEL), lambda i: (i, 0)),
            pl.BlockSpec((tm, D_MODEL), lambda i: (i, 0)),
        ],
        out_shape=[
            jax.ShapeDtypeStruct((t, D_MODEL), jnp.float32),
            jax.ShapeDtypeStruct((t, D_MODEL), jnp.bfloat16),
        ],
        scratch_shapes=[pltpu.VMEM((tm, B_WIDTH), jnp.bfloat16)],
        compiler_params=pltpu.CompilerParams(
            dimension_semantics=("arbitrary",),
            vmem_limit_bytes=VMEM_LIMIT),
        name="out0",
    )(ya, proj, proj, proj, x2, w_bf, w_bf, g, ln_g, ln_b, w_s, b_st)


def _proj1_kernel(xn_ref, w_ref, o_ref):
    def proj():
        return jnp.dot(xn_ref[...].astype(w_ref.dtype), w_ref[...],
                       preferred_element_type=jnp.float32)

    @pl.when(pl.program_id(0) == 0)
    def _():
        o_ref[...] = proj().astype(o_ref.dtype)

    @pl.when(pl.program_id(0) == 1)
    def _():
        o_ref[...] = _silu(proj()).astype(o_ref.dtype)


def _proj1(xn, w_bf, *, tm=512):
    t = xn.shape[0]
    return pl.pallas_call(
        _proj1_kernel,
        grid=(2, t // tm),
        in_specs=[
            pl.BlockSpec((tm, D_MODEL), lambda j, i: (i, 0)),
            pl.BlockSpec((D_MODEL, C_WIDTH), lambda j, i: (0, j),
                         pipeline_mode=pl.Buffered(1)),
        ],
        out_specs=pl.BlockSpec((tm, C_WIDTH), lambda j, i: (i, j)),
        out_shape=jax.ShapeDtypeStruct((t, 2 * C_WIDTH), jnp.bfloat16),
        compiler_params=pltpu.CompilerParams(
            dimension_semantics=("arbitrary", "arbitrary"),
            vmem_limit_bytes=VMEM_LIMIT),
        name="proj1",
    )(xn, w_bf)


def _segment_pitch(seg):
    units = seg // SUBLANES
    return SUBLANES * (units + 1 if units % 2 == 0 else units)


def _build_gate_slabs(wp_ref, wg_sc, tmp_sc):
    row = lax.broadcasted_iota(jnp.int32, (GATE_PAD, LANES), 0)
    col = lax.broadcasted_iota(jnp.int32, (GATE_PAD, LANES), 1)
    for j in range(N_CT):
        tmp_sc[...] = jnp.zeros(tmp_sc.shape, jnp.float32)
        h_lo = (LANES * j) // C_BLOCK
        h_hi = (LANES * j + LANES - 1) // C_BLOCK
        for gate in range(2):
            for h in range(h_lo, h_hi + 1):
                sel = jnp.where(row == col + (LANES * j - C_BLOCK * h), 1.0, 0.0)
                part = jnp.dot(wp_ref[gate, h], sel.astype(jnp.bfloat16),
                               preferred_element_type=jnp.float32)
                r0 = C_BLOCK * h - GATE_WS[j]
                cs = slice(gate * LANES, (gate + 1) * LANES)
                tmp_sc[r0:r0 + C_BLOCK, cs] = tmp_sc[r0:r0 + C_BLOCK, cs] + part[:C_BLOCK, :]
        wg_sc[j] = (0.5 * tmp_sc[...]).astype(jnp.bfloat16)


def _rglru_kernel(xb_ref, zc_ref, cw_ref, cb_ref, wp_ref, ba_ref, bx_ref,
                  lam_ref, o_ref, pad_sc, xc_sc, xcb_sc, ga_sc, gx_sc, h_sc,
                  wg_ref, tmp_sc, hl_sc, p_sc, prm_sc, *, ts, rows):
    si = pl.program_id(1)
    halo = SUBLANES
    vshape = (N_CT, SUBLANES, LANES)

    @pl.when((pl.program_id(0) == 0) & (si == 0))
    def _():
        _build_gate_slabs(wp_ref, wg_ref, tmp_sc)
        lam = lam_ref[...]
        sp = jnp.maximum(-lam, 0.0) + jnp.log1p(jnp.exp(-jnp.abs(lam)))
        prm_sc[0] = jnp.broadcast_to((-0.5 * C_GATE_C) * sp, vshape)
        prm_sc[1] = jnp.broadcast_to(0.5 * ba_ref[...], vshape)
        prm_sc[2] = jnp.broadcast_to(0.5 * bx_ref[...], vshape)

    @pl.when(si == 0)
    def _():
        pad_sc[0:halo, :] = jnp.zeros((halo, C_WIDTH), jnp.float32)
        h_sc[...] = jnp.zeros(h_sc.shape, jnp.float32)

    @pl.when(si > 0)
    def _():
        pad_sc[0:halo, :] = pad_sc[ts:ts + halo, :]

    pad_sc[halo:halo + ts, :] = xb_ref[...].astype(jnp.float32)
    xc = cb_ref[...] + cw_ref[C_CONV - 1:C_CONV, :] * pad_sc[halo:halo + ts, :]
    for tap in range(C_CONV - 1):
        off = halo - (C_CONV - 1) + tap
        xc = xc + cw_ref[tap:tap + 1, :] * pad_sc[off:off + ts, :]
    xcb_sc[...] = xc.astype(jnp.bfloat16)
    seg = ts // SUBLANES
    pitch = _segment_pitch(seg)

    def put(dst_sc, j, val):
        for s in range(SUBLANES):
            dst_sc[j, s * pitch:s * pitch + seg, :] = val[s * seg:(s + 1) * seg, :]

    for j in range(N_CT):
        put(xc_sc, j, xc[:, j * LANES:(j + 1) * LANES])

    for j in range(N_CT):
        ws = GATE_WS[j]
        g = jnp.dot(xcb_sc[:, ws:ws + GATE_WIN], wg_ref[j],
                    preferred_element_type=jnp.float32)
        put(ga_sc, j, g[:, :LANES])
        put(gx_sc, j, g[:, LANES:])

    blocks_per_seg = seg // rows

    def padded_rows(it):
        start = (it // blocks_per_seg) * pitch + (it % blocks_per_seg) * rows
        return pl.ds(pl.multiple_of(start, SUBLANES), rows)

    def local(i, carry):
        h, p = carry
        rs = pl.ds(i, SUBLANES, stride=pitch)
        half_c = prm_sc[0]
        tr = jnp.tanh(ga_sc[:, rs, :] + prm_sc[1])
        ti = jnp.tanh(gx_sc[:, rs, :] + prm_sc[2])
        log_a = half_c + half_c * tr
        a = jnp.exp(log_a)
        half_a = 0.5 * a
        y = (-0.25 - half_a * half_a) * jnp.tanh(log_a)
        half_beta = jnp.where(y > 0.0, y * lax.rsqrt(y), 0.0)
        x = xc_sc[:, rs, :]
        b = half_beta * (x + x * ti)
        h = a * h + b
        p = a * p
        blk = pl.ds(pl.multiple_of(i * SUBLANES, SUBLANES), SUBLANES)
        hl_sc[:, blk, :] = h
        p_sc[:, blk, :] = p
        return h, p

    h_end, p_end = lax.fori_loop(
        0, seg, local,
        (jnp.zeros(vshape, jnp.float32), jnp.ones(vshape, jnp.float32)), unroll=2)

    sub = lax.broadcasted_iota(jnp.int32, vshape, 1)
    a, b = p_end, h_end
    for d in (1, 2, 4):
        keep = sub >= d
        a_sh = pltpu.roll(a, d, 1)
        b_sh = pltpu.roll(b, d, 1)
        b = jnp.where(keep, a * b_sh + b, b)
        a = jnp.where(keep, a * a_sh, a)
    carry_in = h_sc[...]
    seg_end = a * carry_in + b
    h_in = jnp.where(sub == 0, carry_in, pltpu.roll(seg_end, 1, 1))
    h_sc[...] = jnp.broadcast_to(seg_end[:, SUBLANES - 1:SUBLANES, :], vshape)

    def fix(i, c):
        blk = pl.ds(pl.multiple_of(i * SUBLANES, SUBLANES), SUBLANES)
        xc_sc[:, pl.ds(i, SUBLANES, stride=pitch), :] = hl_sc[:, blk, :] + p_sc[:, blk, :] * h_in
        return c

    lax.fori_loop(0, seg, fix, 0)

    def emit(it, c):
        rs = pl.ds(pl.multiple_of(it * rows, rows), rows)
        ps = padded_rows(it)
        h = jnp.concatenate([xc_sc[j, ps, :] for j in range(N_CT)], axis=-1)
        o_ref[rs, :] = (h * zc_ref[rs, :].astype(jnp.float32)).astype(o_ref.dtype)
        return c

    lax.fori_loop(0, ts // rows, emit, 0)


def _rglru(xbz3, cw, cb, wp, ba, bx, lam, *, ts=256, rows=16):
    b, s, _ = xbz3.shape
    const = pl.Buffered(1)
    vec = lambda: pl.BlockSpec((N_CT, 1, LANES), lambda bi, si: (0, 0, 0))
    tiled = lambda: pltpu.VMEM((N_CT, ts, LANES), jnp.float32)
    padded = lambda: pltpu.VMEM(
        (N_CT, SUBLANES * _segment_pitch(ts // SUBLANES), LANES), jnp.float32)
    tile_major = lambda v: v.reshape(N_CT, 1, LANES)
    return pl.pallas_call(
        functools.partial(_rglru_kernel, ts=ts, rows=rows),
        grid=(b, s // ts),
        in_specs=[
            pl.BlockSpec((None, ts, C_WIDTH), lambda bi, si: (bi, si, 0)),
            pl.BlockSpec((None, ts, C_WIDTH), lambda bi, si: (bi, si, 1)),
            pl.BlockSpec((C_CONV, C_WIDTH), lambda bi, si: (0, 0)),
            pl.BlockSpec((1, C_WIDTH), lambda bi, si: (0, 0)),
            pl.BlockSpec((2, C_HEADS, GATE_PAD, GATE_PAD), lambda bi, si: (0, 0, 0, 0),
                         pipeline_mode=const),
            vec(), vec(), vec(),
        ],
        out_specs=pl.BlockSpec((None, ts, C_WIDTH), lambda bi, si: (bi, si, 0)),
        out_shape=jax.ShapeDtypeStruct((b, s, C_WIDTH), jnp.bfloat16),
        scratch_shapes=[
            pltpu.VMEM((ts + SUBLANES, C_WIDTH), jnp.float32),
            padded(),
            pltpu.VMEM((ts, C_WIDTH), jnp.bfloat16),
            padded(),
            padded(),
            pltpu.VMEM((N_CT, SUBLANES, LANES), jnp.float32),
            pltpu.VMEM((N_CT, GATE_WIN, 2 * LANES), jnp.bfloat16),
            pltpu.VMEM((GATE_WIN, 2 * LANES), jnp.float32),
            tiled(),
            tiled(),
            pltpu.VMEM((3, N_CT, SUBLANES, LANES), jnp.float32),
        ],
        compiler_params=pltpu.CompilerParams(
            dimension_semantics=("arbitrary", "arbitrary"),
            vmem_limit_bytes=VMEM_LIMIT),
        name="rglru",
    )(xbz3, xbz3, cw, cb, wp, tile_major(ba), tile_major(bx), tile_major(lam))


def _out1_kernel(hz_ref, x_ref, w_ref, g_ref, o_ref):
    y = jnp.dot(hz_ref[...].astype(w_ref.dtype), w_ref[...],
                preferred_element_type=jnp.float32)
    o_ref[...] = _rms_rows(x_ref[...] + y, g_ref[...])


def _out1(hz, x1, w_bf, g, *, tm=256):
    t = x1.shape[0]
    return pl.pallas_call(
        _out1_kernel,
        grid=(t // tm,),
        in_specs=[
            pl.BlockSpec((tm, C_WIDTH), lambda i: (i, 0)),
            pl.BlockSpec((tm, D_MODEL), lambda i: (i, 0)),
            pl.BlockSpec((C_WIDTH, D_MODEL), lambda i: (0, 0),
                         pipeline_mode=pl.Buffered(1)),
            pl.BlockSpec((1, D_MODEL), lambda i: (0, 0)),
        ],
        out_specs=pl.BlockSpec((tm, D_MODEL), lambda i: (i, 0)),
        out_shape=jax.ShapeDtypeStruct((t, D_MODEL), jnp.float32),
        compiler_params=pltpu.CompilerParams(
            dimension_semantics=("arbitrary",),
            vmem_limit_bytes=VMEM_LIMIT),
        name="out1",
    )(hz, x1, w_bf, g)


def kernel(x, ab_norm, ab_w_in, ab_lambda, ab_head_norm, ab_sgu_ln_g, ab_sgu_ln_b, ab_sgu_w, ab_sgu_b, ab_w_out, c_norm, c_w_in, c_conv_w, c_conv_b, c_gate_a_w, c_gate_a_b, c_gate_x_w, c_gate_x_b, c_lambda, c_w_out, final_norm):
    bsz, seq, d = x.shape
    t = bsz * seq
    bf = jnp.bfloat16
    row = lambda v: v.reshape(1, -1)
    x2 = x.reshape(t, d)

    proj = _proj0(x2, row(ab_norm[0]), ab_w_in[0])
    ya = _attention(proj.reshape(bsz, seq, AB_IN), ab_lambda[0], row(ab_head_norm[0]))
    x1, xn1 = _out0(ya.reshape(t, A_WIDTH), proj, x2, ab_w_out[0].astype(bf), row(c_norm[0]),
                    row(ab_sgu_ln_g[0]), row(ab_sgu_ln_b[0]), ab_sgu_w[0], ab_sgu_b[0].T)

    xbz = _proj1(xn1, c_w_in[0])
    padw = GATE_PAD - C_BLOCK
    wp = jnp.pad(jnp.stack([c_gate_a_w[0], c_gate_x_w[0]]).astype(bf),
                 ((0, 0), (0, 0), (0, padw), (0, padw)))
    hz = _rglru(xbz.reshape(bsz, seq, 2 * C_WIDTH), c_conv_w[0], row(c_conv_b[0]), wp,
                row(c_gate_a_b[0]), row(c_gate_x_b[0]), row(c_lambda[0]))
    out = _out1(hz.reshape(t, C_WIDTH), x1, c_w_out[0], row(final_norm))
    return out.reshape(bsz, seq, d)
```

```python
import functools
import math

import jax
import jax.numpy as jnp
import numpy as np
from jax import lax
from jax.experimental import pallas as pl
from jax.experimental.pallas import tpu as pltpu

D_MODEL = 2048
CHUNK = 64
CHUNK_SHIFT = 6
NORM_EPS = 1e-6
A_HEADS = 8
A_HEAD_DIM = 128
A_VDIM = 2 * A_HEAD_DIM
A_WIDTH = A_HEADS * A_VDIM
B_GROUPS = 8
B_WIDTH = D_MODEL
B_GROUP_DIM = B_WIDTH // B_GROUPS
B_SPAN = 128
C_WIDTH = 2688
C_HEADS = 16
C_BLOCK = C_WIDTH // C_HEADS
C_CONV = 4
C_GATE_C = 8.0
AB_IN = 4 * A_WIDTH + 3 * B_WIDTH
LAM_INIT_0 = 0.8 - 0.6 * math.exp(-0.3 * 0)

LANES = 128
SUBLANES = 8
LOG2E = 1.4426950408889634
VMEM_LIMIT = 56 * 1024 * 1024
GATE_WIN = 512
GATE_PAD = 256
N_CT = C_WIDTH // LANES


def _gate_window_start(j):
    h0 = (LANES * j) // C_BLOCK
    ws = (C_BLOCK * h0) // LANES * LANES
    return min(ws, C_WIDTH - GATE_WIN)


GATE_WS = tuple(_gate_window_start(j) for j in range(N_CT))


def _silu(z):
    h = 0.5 * z
    return h + h * jnp.tanh(h)


def _gelu_tanh(x):
    c = math.sqrt(2.0 / math.pi)
    h = 0.5 * x
    return h + h * jnp.tanh(x * (c + (c * 0.044715) * (x * x)))


def _rms_rows(xf, g):
    ms = jnp.mean(xf * xf, axis=-1, keepdims=True)
    return xf * lax.rsqrt(ms + NORM_EPS) * g


def _proj0_kernel(x_ref, g_ref, w_ref, *rest, tiles_per_seg, emit_weights):
    if emit_weights:
        o_ref, wb_ref, xn_sc = rest
        wb_ref[...] = w_ref[...].astype(wb_ref.dtype)
    else:
        _, o_ref, xn_sc = rest
    j = pl.program_id(1)

    @pl.when(j == 0)
    def _():
        xn_sc[...] = _rms_rows(x_ref[...], g_ref[...]).astype(xn_sc.dtype)

    seg = j // tiles_per_seg

    def proj():
        return jnp.dot(xn_sc[...], w_ref[...], preferred_element_type=jnp.float32)

    @pl.when(seg <= 2)
    def _():
        scale = jnp.where(seg == 0, A_HEAD_DIM ** -0.5 * LOG2E, 1.0).astype(jnp.float32)
        o_ref[...] = (proj() * scale).astype(o_ref.dtype)

    @pl.when((seg == 3) | (seg == 6))
    def _():
        o_ref[...] = _silu(proj()).astype(o_ref.dtype)

    @pl.when((seg == 4) | (seg == 5))
    def _():
        o_ref[...] = _gelu_tanh(proj()).astype(o_ref.dtype)


def _proj0(x2, g, w, *, tm=1024, tn=1024):
    t = x2.shape[0]
    n = w.shape[1]
    params = pltpu.CompilerParams(
        dimension_semantics=("arbitrary", "arbitrary"), vmem_limit_bytes=VMEM_LIMIT)
    x_spec = lambda first: pl.BlockSpec((tm, D_MODEL), lambda i, j: (i + first, 0))
    g_spec = pl.BlockSpec((1, D_MODEL), lambda i, j: (0, 0))
    w_spec = pl.BlockSpec((D_MODEL, tn), lambda i, j: (0, j))
    o_spec = lambda first: pl.BlockSpec((tm, tn), lambda i, j: (i + first, j))
    proj_shape = jax.ShapeDtypeStruct((t, n), jnp.bfloat16)
    proj, w_bf = pl.pallas_call(
        functools.partial(_proj0_kernel, tiles_per_seg=A_WIDTH // tn, emit_weights=True),
        grid=(1, n // tn),
        in_specs=[x_spec(0), g_spec, w_spec],
        out_specs=[o_spec(0), w_spec],
        out_shape=[proj_shape, jax.ShapeDtypeStruct(w.shape, jnp.bfloat16)],
        scratch_shapes=[pltpu.VMEM((tm, D_MODEL), w.dtype)],
        compiler_params=params,
        name="proj0_head",
    )(x2, g, w)
    return pl.pallas_call(
        functools.partial(_proj0_kernel, tiles_per_seg=A_WIDTH // tn, emit_weights=False),
        grid=(t // tm - 1, n // tn),
        in_specs=[x_spec(1), g_spec, w_spec, pl.BlockSpec(memory_space=pl.ANY)],
        out_specs=o_spec(1),
        out_shape=proj_shape,
        input_output_aliases={3: 0},
        scratch_shapes=[pltpu.VMEM((tm, D_MODEL), jnp.bfloat16)],
        compiler_params=params,
        name="proj0",
    )(x2, g, w_bf, proj)


ATT_WIDE = 1


def _attn_schedule(nq):
    wide = [(i, j, ATT_WIDE, False)
            for j in range(0, nq, ATT_WIDE) for i in range(j + ATT_WIDE, nq)]
    narrow = [(i, j, 1, False)
              for i in range(nq) for j in range(i - i % ATT_WIDE, i)]
    diag = [(i, i, 1, True) for i in range(nq)]
    steps = wide + narrow + diag
    assert len(steps) % 2 == 0
    first_visit = {}
    for n, st in enumerate(steps):
        first_visit.setdefault(st[0], n)
    flags = []
    for n, (i, _, w, masked) in enumerate(steps):
        nxt = steps[n + 1][2] if n + 1 < len(steps) else 0
        flags.append((w, masked, first_visit[i] == n, nxt))
    pairs = [(flags[n], flags[n + 1]) for n in range(0, len(steps), 2)]
    runs = []
    for k, pf in enumerate(pairs):
        if runs and runs[-1][2] == pf:
            runs[-1] = (runs[-1][0], runs[-1][1] + 1, pf)
        else:
            runs.append((k, 1, pf))
    return steps, tuple(runs)


def _attn_kernel(it_ref, jt_ref, lamp_ref, q_ref, k_ref, v_ref, za_ref, hn_ref, o_ref,
                 m_sc, l_sc, acc_sc, s0_sc, s1_sc, *, t, pair_runs):
    s_bufs = (s0_sc, s1_sc)
    nt = (((1,), (1,)), ((), ()))

    lp = lamp_ref[...]
    d1 = jnp.sum(lp[0:1, :] * lp[1:2, :], axis=1, keepdims=True)
    d2 = jnp.sum(lp[2:3, :] * lp[3:4, :], axis=1, keepdims=True)
    lam = jnp.exp(d1) - jnp.exp(d2) + LAM_INIT_0

    def rows_of(tile, tiles=1):
        return pl.ds(pl.multiple_of(tile * t, t), tiles * t)

    def scores(n, width, s_ref):
        qs, ks = rows_of(it_ref[n]), rows_of(jt_ref[n], width)
        s_ref[0:t, 0:width * t] = lax.dot_general(
            q_ref[qs, :A_HEAD_DIM], k_ref[ks, :A_HEAD_DIM], nt,
            preferred_element_type=jnp.float32)
        s_ref[t:2 * t, 0:width * t] = lax.dot_general(
            q_ref[qs, A_HEAD_DIM:], k_ref[ks, A_HEAD_DIM:], nt,
            preferred_element_type=jnp.float32)

    def softmax_pv(n, s_ref, width, masked, first):
        i = it_ref[n]
        keys = width * t
        s = s_ref[:, 0:keys]
        if masked:
            rq = lax.broadcasted_iota(jnp.int32, (2 * t, 1), 0)
            rq = lax.shift_right_logical(jnp.where(rq >= t, rq - t, rq), CHUNK_SHIFT)
            ck = lax.shift_right_logical(
                lax.broadcasted_iota(jnp.int32, (1, keys), 1), CHUNK_SHIFT)
            s = jnp.where(ck <= rq, s, -jnp.inf)
        v = v_ref[rows_of(jt_ref[n], width), :]
        if first:
            m_new = jnp.broadcast_to(jnp.max(s, axis=1, keepdims=True), (2 * t, LANES))
            p = jnp.exp2(s - pltpu.repeat(m_new, keys // LANES, 1))
            l_sc[i] = jnp.broadcast_to(jnp.sum(p, axis=1, keepdims=True), (2 * t, LANES))
            m_sc[i] = m_new
            acc_sc[i] = jnp.dot(p.astype(jnp.bfloat16), v, preferred_element_type=jnp.float32)
            return
        m_prev = m_sc[i]
        m_new = jnp.maximum(m_prev, jnp.max(s, axis=1, keepdims=True))
        alpha = jnp.exp2(m_prev - m_new)
        p = jnp.exp2(s - pltpu.repeat(m_new, keys // LANES, 1))
        l_sc[i] = alpha * l_sc[i] + jnp.sum(p, axis=1, keepdims=True)
        m_sc[i] = m_new
        pv = jnp.dot(p.astype(jnp.bfloat16), v,
                     preferred_element_type=jnp.float32)
        acc_sc[i] = acc_sc[i] * pltpu.repeat(alpha, A_VDIM // LANES, 1) + pv

    def finish(i):
        inv_l = 1.0 / l_sc[i]
        acc = acc_sc[i] * pltpu.repeat(inv_l, A_VDIM // LANES, 1)
        o = acc[:t, :] - lam * acc[t:, :]
        ms = jnp.mean(o * o, axis=-1, keepdims=True)
        o = o * lax.rsqrt(ms + NORM_EPS) * hn_ref[...] * (1.0 - LAM_INIT_0)
        rows = rows_of(i)
        o_ref[rows, :] = (o * za_ref[rows, :].astype(jnp.float32)).astype(o_ref.dtype)

    def tick(n, par, width, masked, first, next_width):
        if next_width:
            scores(n + 1, next_width, s_bufs[1 - par])
        softmax_pv(n, s_bufs[par], width, masked, first)
        if masked:
            finish(it_ref[n])

    scores(0, pair_runs[0][2][0][0], s0_sc)
    for start, count, flags in pair_runs:
        def pair(pp, c, start=start, flags=flags):
            n = 2 * (start + pp)
            tick(n, 0, *flags[0])
            tick(n + 1, 1, *flags[1])
            return c
        if count == 1:
            pair(0, 0)
        else:
            lax.fori_loop(0, count, pair, 0)


def _attention(proj3, lamp, hn, *, t=512):
    b, s, _ = proj3.shape
    nblk = A_WIDTH // A_VDIM
    nq = s // t
    steps, pair_runs = _attn_schedule(nq)
    it = jnp.asarray([st[0] for st in steps], jnp.int32)
    jt = jnp.asarray([st[1] for st in steps], jnp.int32)
    head = lambda blk: pl.BlockSpec((None, s, A_VDIM),
                                    lambda bi, h, it_r, jt_r: (bi, 0, blk * nblk + h))
    grid_spec = pltpu.PrefetchScalarGridSpec(
        num_scalar_prefetch=2,
        grid=(b, A_HEADS),
        in_specs=[
            pl.BlockSpec((4, A_HEAD_DIM), lambda bi, h, it_r, jt_r: (0, 0)),
            head(0), head(1), head(2), head(3),
            pl.BlockSpec((1, A_VDIM), lambda bi, h, it_r, jt_r: (0, h)),
        ],
        out_specs=pl.BlockSpec((None, s, A_VDIM), lambda bi, h, it_r, jt_r: (bi, 0, h)),
        scratch_shapes=[
            pltpu.VMEM((nq, 2 * t, LANES), jnp.float32),
            pltpu.VMEM((nq, 2 * t, LANES), jnp.float32),
            pltpu.VMEM((nq, 2 * t, A_VDIM), jnp.float32),
            pltpu.VMEM((2 * t, ATT_WIDE * t), jnp.float32),
            pltpu.VMEM((2 * t, ATT_WIDE * t), jnp.float32),
        ],
    )
    return pl.pallas_call(
        functools.partial(_attn_kernel, t=t, pair_runs=pair_runs),
        grid_spec=grid_spec,
        out_shape=jax.ShapeDtypeStruct((b, s, A_WIDTH), jnp.bfloat16),
        compiler_params=pltpu.CompilerParams(
            dimension_semantics=("arbitrary", "arbitrary"),
            vmem_limit_bytes=VMEM_LIMIT),
        name="diff_attn",
    )(it, jt, lamp, proj3, proj3, proj3, proj3, hn)


def _sgu_tile(gu_ref, gv_ref, zb_ref, lng_ref, lnb_ref, ws_ref, bst_ref, yb_sc, rows):
    v = gv_ref[...].astype(jnp.float32)
    mu = jnp.mean(v, axis=-1, keepdims=True)
    vc = v - mu
    var = jnp.mean(vc * vc, axis=-1, keepdims=True)
    vln = (vc * lax.rsqrt(var + NORM_EPS) * lng_ref[...] + lnb_ref[...])
    vln = vln.astype(jnp.bfloat16)
    ci = lax.broadcasted_iota(jnp.int32, (B_SPAN, B_SPAN), 0) // CHUNK
    cj = lax.broadcasted_iota(jnp.int32, (B_SPAN, B_SPAN), 1) // CHUNK
    keep = ci >= cj
    for g in range(B_GROUPS):
        w = jnp.where(keep, ws_ref[g], 0.0).astype(jnp.bfloat16)
        bcol = bst_ref[:, g:g + 1]
        c0 = g * B_GROUP_DIM
        for sp in range(rows // B_SPAN):
            r0 = sp * B_SPAN
            mixed = jnp.dot(w, vln[r0:r0 + B_SPAN, c0:c0 + B_GROUP_DIM],
                            preferred_element_type=jnp.float32) + bcol
            u = gu_ref[r0:r0 + B_SPAN, c0:c0 + B_GROUP_DIM].astype(jnp.float32)
            z = zb_ref[r0:r0 + B_SPAN, c0:c0 + B_GROUP_DIM].astype(jnp.float32)
            yb_sc[r0:r0 + B_SPAN, c0:c0 + B_GROUP_DIM] = (u * mixed * z).astype(yb_sc.dtype)


def _out0_kernel(ya_ref, gu_ref, gv_ref, zb_ref, x_ref, wa_ref, wb_ref, g_ref,
                 lng_ref, lnb_ref, ws_ref, bst_ref, x1_ref, xn_ref, yb_sc, *, rows):
    y = jnp.dot(ya_ref[...], wa_ref[...], preferred_element_type=jnp.float32)
    _sgu_tile(gu_ref, gv_ref, zb_ref, lng_ref, lnb_ref, ws_ref, bst_ref, yb_sc, rows)
    y = y + jnp.dot(yb_sc[...], wb_ref[...], preferred_element_type=jnp.float32)
    x1 = x_ref[...] + y
    x1_ref[...] = x1
    xn_ref[...] = _rms_rows(x1, g_ref[...]).astype(xn_ref.dtype)


def _out0(ya, proj, x2, w_bf, g, ln_g, ln_b, w_s, b_st, *, tm=256):
    t = x2.shape[0]
    const = pl.Buffered(1)
    base = 4 * A_WIDTH // B_WIDTH
    vec = lambda n: pl.BlockSpec((1, n), lambda i: (0, 0))
    return pl.pallas_call(
        functools.partial(_out0_kernel, rows=tm),
        grid=(t // tm,),
        in_specs=[
            pl.BlockSpec((tm, A_WIDTH), lambda i: (i, 0)),
            pl.BlockSpec((tm, B_WIDTH), lambda i: (i, base)),
            pl.BlockSpec((tm, B_WIDTH), lambda i: (i, base + 1)),
            pl.BlockSpec((tm, B_WIDTH), lambda i: (i, base + 2)),
            pl.BlockSpec((tm, D_MODEL), lambda i: (i, 0)),
            pl.BlockSpec((A_WIDTH, D_MODEL), lambda i: (0, 0), pipeline_mode=const),
            pl.BlockSpec((B_WIDTH, D_MODEL), lambda i: (1, 0), pipeline_mode=const),
            vec(D_MODEL), vec(B_WIDTH), vec(B_WIDTH),
            pl.BlockSpec((B_GROUPS, B_SPAN, B_SPAN), lambda i: (0, 0, 0)),
            pl.BlockSpec((B_SPAN, B_GROUPS), lambda i: (0, 0)),
        ],
        out_specs=[
            pl.BlockSpec((tm, D_MODEL), lambda i: (i, 0)),
            pl.BlockSpec((tm, D_MODEL), lambda i: (i, 0)),
        ],
        out_shape=[
            jax.ShapeDtypeStruct((t, D_MODEL), jnp.float32),
            jax.ShapeDtypeStruct((t, D_MODEL), jnp.bfloat16),
        ],
        scratch_shapes=[pltpu.VMEM((tm, B_WIDTH), jnp.bfloat16)],
        compiler_params=pltpu.CompilerParams(
            dimension_semantics=("arbitrary",),
            vmem_limit_bytes=VMEM_LIMIT),
        name="out0",
    )(ya, proj, proj, proj, x2, w_bf, w_bf, g, ln_g, ln_b, w_s, b_st)


def _proj1_kernel(xn_ref, w_ref, o_ref):
    def proj():
        return jnp.dot(xn_ref[...].astype(w_ref.dtype), w_ref[...],
                       preferred_element_type=jnp.float32)

    @pl.when(pl.program_id(0) == 0)
    def _():
        o_ref[...] = proj().astype(o_ref.dtype)

    @pl.when(pl.program_id(0) == 1)
    def _():
        o_ref[...] = _silu(proj()).astype(o_ref.dtype)


def _proj1(xn, w_bf, *, tm=512):
    t = xn.shape[0]
    return pl.pallas_call(
        _proj1_kernel,
        grid=(2, t // tm),
        in_specs=[
            pl.BlockSpec((tm, D_MODEL), lambda j, i: (i, 0)),
            pl.BlockSpec((D_MODEL, C_WIDTH), lambda j, i: (0, j),
                         pipeline_mode=pl.Buffered(1)),
        ],
        out_specs=pl.BlockSpec((tm, C_WIDTH), lambda j, i: (i, j)),
        out_shape=jax.ShapeDtypeStruct((t, 2 * C_WIDTH), jnp.bfloat16),
        compiler_params=pltpu.CompilerParams(
            dimension_semantics=("arbitrary", "arbitrary"),
            vmem_limit_bytes=VMEM_LIMIT),
        name="proj1",
    )(xn, w_bf)


def _segment_pitch(seg):
    units = seg // SUBLANES
    return SUBLANES * (units + 1 if units % 2 == 0 else units)


def _build_gate_slabs(wp_ref, wg_sc, tmp_sc):
    row = lax.broadcasted_iota(jnp.int32, (GATE_PAD, LANES), 0)
    col = lax.broadcasted_iota(jnp.int32, (GATE_PAD, LANES), 1)
    for j in range(N_CT):
        tmp_sc[...] = jnp.zeros(tmp_sc.shape, jnp.float32)
        h_lo = (LANES * j) // C_BLOCK
        h_hi = (LANES * j + LANES - 1) // C_BLOCK
        for gate in range(2):
            for h in range(h_lo, h_hi + 1):
                sel = jnp.where(row == col + (LANES * j - C_BLOCK * h), 1.0, 0.0)
                part = jnp.dot(wp_ref[gate, h], sel.astype(jnp.bfloat16),
                               preferred_element_type=jnp.float32)
                r0 = C_BLOCK * h - GATE_WS[j]
                cs = slice(gate * LANES, (gate + 1) * LANES)
                tmp_sc[r0:r0 + C_BLOCK, cs] = tmp_sc[r0:r0 + C_BLOCK, cs] + part[:C_BLOCK, :]
        wg_sc[j] = (0.5 * tmp_sc[...]).astype(jnp.bfloat16)


def _rglru_kernel(xb_ref, zc_ref, cw_ref, cb_ref, wp_ref, ba_ref, bx_ref,
                  lam_ref, o_ref, pad_sc, xc_sc, xcb_sc, ga_sc, gx_sc, h_sc,
                  wg_ref, tmp_sc, hl_sc, p_sc, prm_sc, *, ts, rows):
    si = pl.program_id(1)
    halo = SUBLANES
    vshape = (N_CT, SUBLANES, LANES)

    @pl.when((pl.program_id(0) == 0) & (si == 0))
    def _():
        _build_gate_slabs(wp_ref, wg_ref, tmp_sc)
        lam = lam_ref[...]
        sp = jnp.maximum(-lam, 0.0) + jnp.log1p(jnp.exp(-jnp.abs(lam)))
        prm_sc[0] = jnp.broadcast_to((-0.5 * C_GATE_C) * sp, vshape)
        prm_sc[1] = jnp.broadcast_to(0.5 * ba_ref[...], vshape)
        prm_sc[2] = jnp.broadcast_to(0.5 * bx_ref[...], vshape)

    @pl.when(si == 0)
    def _():
        pad_sc[0:halo, :] = jnp.zeros((halo, C_WIDTH), jnp.float32)
        h_sc[...] = jnp.zeros(h_sc.shape, jnp.float32)

    @pl.when(si > 0)
    def _():
        pad_sc[0:halo, :] = pad_sc[ts:ts + halo, :]

    pad_sc[halo:halo + ts, :] = xb_ref[...].astype(jnp.float32)
    xc = cb_ref[...] + cw_ref[C_CONV - 1:C_CONV, :] * pad_sc[halo:halo + ts, :]
    for tap in range(C_CONV - 1):
        off = halo - (C_CONV - 1) + tap
        xc = xc + cw_ref[tap:tap + 1, :] * pad_sc[off:off + ts, :]
    xcb_sc[...] = xc.astype(jnp.bfloat16)
    seg = ts // SUBLANES
    pitch = _segment_pitch(seg)

    def put(dst_sc, j, val):
        for s in range(SUBLANES):
            dst_sc[j, s * pitch:s * pitch + seg, :] = val[s * seg:(s + 1) * seg, :]

    for j in range(N_CT):
        put(xc_sc, j, xc[:, j * LANES:(j + 1) * LANES])

    for j in range(N_CT):
        ws = GATE_WS[j]
        g = jnp.dot(xcb_sc[:, ws:ws + GATE_WIN], wg_ref[j],
                    preferred_element_type=jnp.float32)
        put(ga_sc, j, g[:, :LANES])
        put(gx_sc, j, g[:, LANES:])

    blocks_per_seg = seg // rows

    def padded_rows(it):
        start = (it // blocks_per_seg) * pitch + (it % blocks_per_seg) * rows
        return pl.ds(pl.multiple_of(start, SUBLANES), rows)

    def local(i, carry):
        h, p = carry
        rs = pl.ds(i, SUBLANES, stride=pitch)
        half_c = prm_sc[0]
        tr = jnp.tanh(ga_sc[:, rs, :] + prm_sc[1])
        ti = jnp.tanh(gx_sc[:, rs, :] + prm_sc[2])
        log_a = half_c + half_c * tr
        a = jnp.exp(log_a)
        half_a = 0.5 * a
        y = (-0.25 - half_a * half_a) * jnp.tanh(log_a)
        half_beta = jnp.where(y > 0.0, y * lax.rsqrt(y), 0.0)
        x = xc_sc[:, rs, :]
        b = half_beta * (x + x * ti)
        h = a * h + b
        p = a * p
        blk = pl.ds(pl.multiple_of(i * SUBLANES, SUBLANES), SUBLANES)
        hl_sc[:, blk, :] = h
        p_sc[:, blk, :] = p
        return h, p

    h_end, p_end = lax.fori_loop(
        0, seg, local,
        (jnp.zeros(vshape, jnp.float32), jnp.ones(vshape, jnp.float32)), unroll=2)

    sub = lax.broadcasted_iota(jnp.int32, vshape, 1)
    a, b = p_end, h_end
    for d in (1, 2, 4):
        keep = sub >= d
        a_sh = pltpu.roll(a, d, 1)
        b_sh = pltpu.roll(b, d, 1)
        b = jnp.where(keep, a * b_sh + b, b)
        a = jnp.where(keep, a * a_sh, a)
    carry_in = h_sc[...]
    seg_end = a * carry_in + b
    h_in = jnp.where(sub == 0, carry_in, pltpu.roll(seg_end, 1, 1))
    h_sc[...] = jnp.broadcast_to(seg_end[:, SUBLANES - 1:SUBLANES, :], vshape)

    def fix(i, c):
        blk = pl.ds(pl.multiple_of(i * SUBLANES, SUBLANES), SUBLANES)
        xc_sc[:, pl.ds(i, SUBLANES, stride=pitch), :] = hl_sc[:, blk, :] + p_sc[:, blk, :] * h_in
        return c

    lax.fori_loop(0, seg, fix, 0)

    def emit(it, c):
        rs = pl.ds(pl.multiple_of(it * rows, rows), rows)
        ps = padded_rows(it)
        h = jnp.concatenate([xc_sc[j, ps, :] for j in range(N_CT)], axis=-1)
        o_ref[rs, :] = (h * zc_ref[rs, :].astype(jnp.float32)).astype(o_ref.dtype)
        return c

    lax.fori_loop(0, ts // rows, emit, 0)


def _rglru(xbz3, cw, cb, wp, ba, bx, lam, *, ts=256, rows=16):
    b, s, _ = xbz3.shape
    const = pl.Buffered(1)
    vec = lambda: pl.BlockSpec((N_CT, 1, LANES), lambda bi, si: (0, 0, 0))
    tiled = lambda: pltpu.VMEM((N_CT, ts, LANES), jnp.float32)
    padded = lambda: pltpu.VMEM(
        (N_CT, SUBLANES * _segment_pitch(ts // SUBLANES), LANES), jnp.float32)
    tile_major = lambda v: v.reshape(N_CT, 1, LANES)
    return pl.pallas_call(
        functools.partial(_rglru_kernel, ts=ts, rows=rows),
        grid=(b, s // ts),
        in_specs=[
            pl.BlockSpec((None, ts, C_WIDTH), lambda bi, si: (bi, si, 0)),
            pl.BlockSpec((None, ts, C_WIDTH), lambda bi, si: (bi, si, 1)),
            pl.BlockSpec((C_CONV, C_WIDTH), lambda bi, si: (0, 0)),
            pl.BlockSpec((1, C_WIDTH), lambda bi, si: (0, 0)),
            pl.BlockSpec((2, C_HEADS, GATE_PAD, GATE_PAD), lambda bi, si: (0, 0, 0, 0),
                         pipeline_mode=const),
            vec(), vec(), vec(),
        ],
        out_specs=pl.BlockSpec((None, ts, C_WIDTH), lambda bi, si: (bi, si, 0)),
        out_shape=jax.ShapeDtypeStruct((b, s, C_WIDTH), jnp.bfloat16),
        scratch_shapes=[
            pltpu.VMEM((ts + SUBLANES, C_WIDTH), jnp.float32),
            padded(),
            pltpu.VMEM((ts, C_WIDTH), jnp.bfloat16),
            padded(),
            padded(),
            pltpu.VMEM((N_CT, SUBLANES, LANES), jnp.float32),
            pltpu.VMEM((N_CT, GATE_WIN, 2 * LANES), jnp.bfloat16),
            pltpu.VMEM((GATE_WIN, 2 * LANES), jnp.float32),
            tiled(),
            tiled(),
            pltpu.VMEM((3, N_CT, SUBLANES, LANES), jnp.float32),
        ],
        compiler_params=pltpu.CompilerParams(
            dimension_semantics=("arbitrary", "arbitrary"),
            vmem_limit_bytes=VMEM_LIMIT),
        name="rglru",
    )(xbz3, xbz3, cw, cb, wp, tile_major(ba), tile_major(bx), tile_major(lam))


def _out1_kernel(hz_ref, x_ref, w_ref, g_ref, o_ref):
    y = jnp.dot(hz_ref[...].astype(w_ref.dtype), w_ref[...],
                preferred_element_type=jnp.float32)
    o_ref[...] = _rms_rows(x_ref[...] + y, g_ref[...])


def _out1(hz, x1, w_bf, g, *, tm=256):
    t = x1.shape[0]
    return pl.pallas_call(
        _out1_kernel,
        grid=(t // tm,),
        in_specs=[
            pl.BlockSpec((tm, C_WIDTH), lambda i: (i, 0)),
            pl.BlockSpec((tm, D_MODEL), lambda i: (i, 0)),
            pl.BlockSpec((C_WIDTH, D_MODEL), lambda i: (0, 0),
                         pipeline_mode=pl.Buffered(1)),
            pl.BlockSpec((1, D_MODEL), lambda i: (0, 0)),
        ],
        out_specs=pl.BlockSpec((tm, D_MODEL), lambda i: (i, 0)),
        out_shape=jax.ShapeDtypeStruct((t, D_MODEL), jnp.float32),
        compiler_params=pltpu.CompilerParams(
            dimension_semantics=("arbitrary",),
            vmem_limit_bytes=VMEM_LIMIT),
        name="out1",
    )(hz, x1, w_bf, g)


def kernel(x, ab_norm, ab_w_in, ab_lambda, ab_head_norm, ab_sgu_ln_g, ab_sgu_ln_b, ab_sgu_w, ab_sgu_b, ab_w_out, c_norm, c_w_in, c_conv_w, c_conv_b, c_gate_a_w, c_gate_a_b, c_gate_x_w, c_gate_x_b, c_lambda, c_w_out, final_norm):
    bsz, seq, d = x.shape
    t = bsz * seq
    bf = jnp.bfloat16
    row = lambda v: v.reshape(1, -1)
    x2 = x.reshape(t, d)

    proj = _proj0(x2, row(ab_norm[0]), ab_w_in[0])
    ya = _attention(proj.reshape(bsz, seq, AB_IN), ab_lambda[0], row(ab_head_norm[0]))
    x1, xn1 = _out0(ya.reshape(t, A_WIDTH), proj, x2, ab_w_out[0].astype(bf), row(c_norm[0]),
                    row(ab_sgu_ln_g[0]), row(ab_sgu_ln_b[0]), ab_sgu_w[0], ab_sgu_b[0].T)

    xbz = _proj1(xn1, c_w_in[0])
    padw = GATE_PAD - C_BLOCK
    wp = jnp.pad(jnp.stack([c_gate_a_w[0], c_gate_x_w[0]]).astype(bf),
                 ((0, 0), (0, 0), (0, padw), (0, padw)))
    hz = _rglru(xbz.reshape(bsz, seq, 2 * C_WIDTH), c_conv_w[0], row(c_conv_b[0]), wp,
                row(c_gate_a_b[0]), row(c_gate_x_b[0]), row(c_lambda[0]))
    out = _out1(hz.reshape(t, C_WIDTH), x1, c_w_out[0], row(final_norm))
    return out.reshape(bsz, seq, d)
```

```python
import functools
import math

import jax
import jax.numpy as jnp
import numpy as np
from jax import lax
from jax.experimental import pallas as pl
from jax.experimental.pallas import tpu as pltpu

D_MODEL = 2048
CHUNK = 64
CHUNK_SHIFT = 6
NORM_EPS = 1e-6
A_HEADS = 8
A_HEAD_DIM = 128
A_VDIM = 2 * A_HEAD_DIM
A_WIDTH = A_HEADS * A_VDIM
B_GROUPS = 8
B_WIDTH = D_MODEL
B_GROUP_DIM = B_WIDTH // B_GROUPS
B_SPAN = 128
C_WIDTH = 2688
C_HEADS = 16
C_BLOCK = C_WIDTH // C_HEADS
C_CONV = 4
C_GATE_C = 8.0
AB_IN = 4 * A_WIDTH + 3 * B_WIDTH
LAM_INIT_0 = 0.8 - 0.6 * math.exp(-0.3 * 0)

LANES = 128
SUBLANES = 8
LOG2E = 1.4426950408889634
VMEM_LIMIT = 56 * 1024 * 1024
SIDE_CAST_STEPS = 64
SIDE_CAST_ROWS = 16
GATE_WIN = 512
GATE_PAD = 256
N_CT = C_WIDTH // LANES


def _gate_window_start(j):
    h0 = (LANES * j) // C_BLOCK
    ws = (C_BLOCK * h0) // LANES * LANES
    return min(ws, C_WIDTH - GATE_WIN)


GATE_WS = tuple(_gate_window_start(j) for j in range(N_CT))


def _silu(z):
    h = 0.5 * z
    return h + h * jnp.tanh(h)


def _gelu_tanh(x):
    c = math.sqrt(2.0 / math.pi)
    h = 0.5 * x
    return h + h * jnp.tanh(x * (c + (c * 0.044715) * (x * x)))


def _rms_rows(xf, g):
    ms = jnp.mean(xf * xf, axis=-1, keepdims=True)
    return xf * lax.rsqrt(ms + NORM_EPS) * g


def _proj0_kernel(x_ref, g_ref, w_ref, *rest, tiles_per_seg, emit_weights, n_side):
    j = pl.program_id(1)
    if emit_weights:
        o_ref, wb_ref, xn_sc = rest
        wb_ref[...] = w_ref[...].astype(wb_ref.dtype)
    else:
        side_in, o_ref = rest[:n_side], rest[n_side + 1]
        side_out, xn_sc = rest[n_side + 2:2 * n_side + 2], rest[-1]
        step = pl.program_id(0) * pl.num_programs(1) + j

        @pl.when(step < SIDE_CAST_STEPS)
        def _():
            for src, dst in zip(side_in, side_out):
                dst[...] = src[...].astype(dst.dtype)

    @pl.when(j == 0)
    def _():
        xn_sc[...] = _rms_rows(x_ref[...], g_ref[...]).astype(xn_sc.dtype)

    seg = j // tiles_per_seg

    def proj():
        return jnp.dot(xn_sc[...], w_ref[...], preferred_element_type=jnp.float32)

    @pl.when(seg <= 2)
    def _():
        scale = jnp.where(seg == 0, A_HEAD_DIM ** -0.5 * LOG2E, 1.0).astype(jnp.float32)
        o_ref[...] = (proj() * scale).astype(o_ref.dtype)

    @pl.when((seg == 3) | (seg == 6))
    def _():
        o_ref[...] = _silu(proj()).astype(o_ref.dtype)

    @pl.when((seg == 4) | (seg == 5))
    def _():
        o_ref[...] = _gelu_tanh(proj()).astype(o_ref.dtype)


def _proj0(x2, g, w, later_weights, *, tm=1024, tn=1024):
    t = x2.shape[0]
    n = w.shape[1]
    grid_body = (t // tm - 1, n // tn)
    assert grid_body[0] * grid_body[1] >= SIDE_CAST_STEPS
    side_rows = SIDE_CAST_STEPS * SIDE_CAST_ROWS
    side = [lw.reshape(side_rows, lw.size // side_rows) for lw in later_weights]
    side_spec = lambda a: pl.BlockSpec(
        (SIDE_CAST_ROWS, a.shape[1]),
        lambda i, j: (jnp.minimum(i * grid_body[1] + j, SIDE_CAST_STEPS - 1), 0))
    params = pltpu.CompilerParams(
        dimension_semantics=("arbitrary", "arbitrary"), vmem_limit_bytes=VMEM_LIMIT)
    x_spec = lambda first: pl.BlockSpec((tm, D_MODEL), lambda i, j: (i + first, 0))
    g_spec = pl.BlockSpec((1, D_MODEL), lambda i, j: (0, 0))
    w_spec = pl.BlockSpec((D_MODEL, tn), lambda i, j: (0, j))
    o_spec = lambda first: pl.BlockSpec((tm, tn), lambda i, j: (i + first, j))
    proj_shape = jax.ShapeDtypeStruct((t, n), jnp.bfloat16)
    proj, w_bf = pl.pallas_call(
        functools.partial(_proj0_kernel, tiles_per_seg=A_WIDTH // tn, emit_weights=True,
                          n_side=0),
        grid=(1, n // tn),
        in_specs=[x_spec(0), g_spec, w_spec],
        out_specs=[o_spec(0), w_spec],
        out_shape=[proj_shape, jax.ShapeDtypeStruct(w.shape, jnp.bfloat16)],
        scratch_shapes=[pltpu.VMEM((tm, D_MODEL), w.dtype)],
        compiler_params=params,
        name="proj0_head",
    )(x2, g, w)
    outs = pl.pallas_call(
        functools.partial(_proj0_kernel, tiles_per_seg=A_WIDTH // tn, emit_weights=False,
                          n_side=len(side)),
        grid=grid_body,
        in_specs=[x_spec(1), g_spec, w_spec] + [side_spec(a) for a in side]
                 + [pl.BlockSpec(memory_space=pl.ANY)],
        out_specs=[o_spec(1)] + [side_spec(a) for a in side],
        out_shape=[proj_shape] + [jax.ShapeDtypeStruct(a.shape, jnp.bfloat16) for a in side],
        input_output_aliases={3 + len(side): 0},
        scratch_shapes=[pltpu.VMEM((tm, D_MODEL), jnp.bfloat16)],
        compiler_params=params,
        name="proj0",
    )(x2, g, w_bf, *side, proj)
    return outs[0], [o.reshape(lw.shape) for o, lw in zip(outs[1:], later_weights)]


ATT_WIDE = 1


def _attn_schedule(nq):
    wide = [(i, j, ATT_WIDE, False)
            for j in range(0, nq, ATT_WIDE) for i in range(j + ATT_WIDE, nq)]
    narrow = [(i, j, 1, False)
              for i in range(nq) for j in range(i - i % ATT_WIDE, i)]
    diag = [(i, i, 1, True) for i in range(nq)]
    steps = wide + narrow + diag
    assert len(steps) % 2 == 0
    first_visit = {}
    for n, st in enumerate(steps):
        first_visit.setdefault(st[0], n)
    flags = []
    for n, (i, _, w, masked) in enumerate(steps):
        nxt = steps[n + 1][2] if n + 1 < len(steps) else 0
        flags.append((w, masked, first_visit[i] == n, nxt))
    pairs = [(flags[n], flags[n + 1]) for n in range(0, len(steps), 2)]
    runs = []
    for k, pf in enumerate(pairs):
        if runs and runs[-1][2] == pf:
            runs[-1] = (runs[-1][0], runs[-1][1] + 1, pf)
        else:
            runs.append((k, 1, pf))
    return steps, tuple(runs)


def _attn_kernel(it_ref, jt_ref, lamp_ref, q_ref, k_ref, v_ref, za_ref, hn_ref, o_ref,
                 m_sc, l_sc, acc_sc, s0_sc, s1_sc, *, t, pair_runs):
    s_bufs = (s0_sc, s1_sc)
    nt = (((1,), (1,)), ((), ()))

    lp = lamp_ref[...]
    d1 = jnp.sum(lp[0:1, :] * lp[1:2, :], axis=1, keepdims=True)
    d2 = jnp.sum(lp[2:3, :] * lp[3:4, :], axis=1, keepdims=True)
    lam = jnp.exp(d1) - jnp.exp(d2) + LAM_INIT_0

    def rows_of(tile, tiles=1):
        return pl.ds(pl.multiple_of(tile * t, t), tiles * t)

    def scores(n, width, s_ref):
        qs, ks = rows_of(it_ref[n]), rows_of(jt_ref[n], width)
        s_ref[0:t, 0:width * t] = lax.dot_general(
            q_ref[qs, :A_HEAD_DIM], k_ref[ks, :A_HEAD_DIM], nt,
            preferred_element_type=jnp.float32)
        s_ref[t:2 * t, 0:width * t] = lax.dot_general(
            q_ref[qs, A_HEAD_DIM:], k_ref[ks, A_HEAD_DIM:], nt,
            preferred_element_type=jnp.float32)

    def softmax_pv(n, s_ref, width, masked, first):
        i = it_ref[n]
        keys = width * t
        s = s_ref[:, 0:keys]
        if masked:
            rq = lax.broadcasted_iota(jnp.int32, (2 * t, 1), 0)
            rq = lax.shift_right_logical(jnp.where(rq >= t, rq - t, rq), CHUNK_SHIFT)
            ck = lax.shift_right_logical(
                lax.broadcasted_iota(jnp.int32, (1, keys), 1), CHUNK_SHIFT)
            s = jnp.where(ck <= rq, s, -jnp.inf)
        v = v_ref[rows_of(jt_ref[n], width), :]
        if first:
            m_new = jnp.broadcast_to(jnp.max(s, axis=1, keepdims=True), (2 * t, LANES))
            p = jnp.exp2(s - pltpu.repeat(m_new, keys // LANES, 1))
            l_sc[i] = jnp.broadcast_to(jnp.sum(p, axis=1, keepdims=True), (2 * t, LANES))
            m_sc[i] = m_new
            acc_sc[i] = jnp.dot(p.astype(jnp.bfloat16), v, preferred_element_type=jnp.float32)
            return
        m_prev = m_sc[i]
        m_new = jnp.maximum(m_prev, jnp.max(s, axis=1, keepdims=True))
        alpha = jnp.exp2(m_prev - m_new)
        p = jnp.exp2(s - pltpu.repeat(m_new, keys // LANES, 1))
        l_sc[i] = alpha * l_sc[i] + jnp.sum(p, axis=1, keepdims=True)
        m_sc[i] = m_new
        pv = jnp.dot(p.astype(jnp.bfloat16), v,
                     preferred_element_type=jnp.float32)
        acc_sc[i] = acc_sc[i] * pltpu.repeat(alpha, A_VDIM // LANES, 1) + pv

    def finish(i):
        inv_l = 1.0 / l_sc[i]
        acc = acc_sc[i] * pltpu.repeat(inv_l, A_VDIM // LANES, 1)
        o = acc[:t, :] - lam * acc[t:, :]
        ms = jnp.mean(o * o, axis=-1, keepdims=True)
        o = o * lax.rsqrt(ms + NORM_EPS) * hn_ref[...] * (1.0 - LAM_INIT_0)
        rows = rows_of(i)
        o_ref[rows, :] = (o * za_ref[rows, :].astype(jnp.float32)).astype(o_ref.dtype)

    def tick(n, par, width, masked, first, next_width):
        if next_width:
            scores(n + 1, next_width, s_bufs[1 - par])
        softmax_pv(n, s_bufs[par], width, masked, first)
        if masked:
            finish(it_ref[n])

    scores(0, pair_runs[0][2][0][0], s0_sc)
    for start, count, flags in pair_runs:
        def pair(pp, c, start=start, flags=flags):
            n = 2 * (start + pp)
            tick(n, 0, *flags[0])
            tick(n + 1, 1, *flags[1])
            return c
        if count == 1:
            pair(0, 0)
        else:
            lax.fori_loop(0, count, pair, 0)


def _attention(proj3, lamp, hn, *, t=512):
    b, s, _ = proj3.shape
    nblk = A_WIDTH // A_VDIM
    nq = s // t
    steps, pair_runs = _attn_schedule(nq)
    it = jnp.asarray([st[0] for st in steps], jnp.int32)
    jt = jnp.asarray([st[1] for st in steps], jnp.int32)
    head = lambda blk: pl.BlockSpec((None, s, A_VDIM),
                                    lambda bi, h, it_r, jt_r: (bi, 0, blk * nblk + h))
    grid_spec = pltpu.PrefetchScalarGridSpec(
        num_scalar_prefetch=2,
        grid=(b, A_HEADS),
        in_specs=[
            pl.BlockSpec((4, A_HEAD_DIM), lambda bi, h, it_r, jt_r: (0, 0)),
            head(0), head(1), head(2), head(3),
            pl.BlockSpec((1, A_VDIM), lambda bi, h, it_r, jt_r: (0, h)),
        ],
        out_specs=pl.BlockSpec((None, s, A_VDIM), lambda bi, h, it_r, jt_r: (bi, 0, h)),
        scratch_shapes=[
            pltpu.VMEM((nq, 2 * t, LANES), jnp.float32),
            pltpu.VMEM((nq, 2 * t, LANES), jnp.float32),
            pltpu.VMEM((nq, 2 * t, A_VDIM), jnp.float32),
            pltpu.VMEM((2 * t, ATT_WIDE * t), jnp.float32),
            pltpu.VMEM((2 * t, ATT_WIDE * t), jnp.float32),
        ],
    )
    return pl.pallas_call(
        functools.partial(_attn_kernel, t=t, pair_runs=pair_runs),
        grid_spec=grid_spec,
        out_shape=jax.ShapeDtypeStruct((b, s, A_WIDTH), jnp.bfloat16),
        compiler_params=pltpu.CompilerParams(
            dimension_semantics=("arbitrary", "arbitrary"),
            vmem_limit_bytes=VMEM_LIMIT),
        name="diff_attn",
    )(it, jt, lamp, proj3, proj3, proj3, proj3, hn)


def _sgu_tile(gu_ref, gv_ref, zb_ref, lng_ref, lnb_ref, ws_ref, bst_ref, yb_sc, rows):
    v = gv_ref[...].astype(jnp.float32)
    mu = jnp.mean(v, axis=-1, keepdims=True)
    vc = v - mu
    var = jnp.mean(vc * vc, axis=-1, keepdims=True)
    vln = (vc * lax.rsqrt(var + NORM_EPS) * lng_ref[...] + lnb_ref[...])
    vln = vln.astype(jnp.bfloat16)
    ci = lax.broadcasted_iota(jnp.int32, (B_SPAN, B_SPAN), 0) // CHUNK
    cj = lax.broadcasted_iota(jnp.int32, (B_SPAN, B_SPAN), 1) // CHUNK
    keep = ci >= cj
    for g in range(B_GROUPS):
        w = jnp.where(keep, ws_ref[g], 0.0).astype(jnp.bfloat16)
        bcol = bst_ref[:, g:g + 1]
        c0 = g * B_GROUP_DIM
        for sp in range(rows // B_SPAN):
            r0 = sp * B_SPAN
            mixed = jnp.dot(w, vln[r0:r0 + B_SPAN, c0:c0 + B_GROUP_DIM],
                            preferred_element_type=jnp.float32) + bcol
            u = gu_ref[r0:r0 + B_SPAN, c0:c0 + B_GROUP_DIM].astype(jnp.float32)
            z = zb_ref[r0:r0 + B_SPAN, c0:c0 + B_GROUP_DIM].astype(jnp.float32)
            yb_sc[r0:r0 + B_SPAN, c0:c0 + B_GROUP_DIM] = (u * mixed * z).astype(yb_sc.dtype)


def _out0_kernel(ya_ref, gu_ref, gv_ref, zb_ref, x_ref, wa_ref, wb_ref, g_ref,
                 lng_ref, lnb_ref, ws_ref, bst_ref, x1_ref, xn_ref, yb_sc, *, rows):
    y = jnp.dot(ya_ref[...], wa_ref[...], preferred_element_type=jnp.float32)
    _sgu_tile(gu_ref, gv_ref, zb_ref, lng_ref, lnb_ref, ws_ref, bst_ref, yb_sc, rows)
    y = y + jnp.dot(yb_sc[...], wb_ref[...], preferred_element_type=jnp.float32)
    x1 = x_ref[...] + y
    x1_ref[...] = x1
    xn_ref[...] = _rms_rows(x1, g_ref[...]).astype(xn_ref.dtype)


def _out0(ya, proj, x2, w_bf, g, ln_g, ln_b, w_s, b_st, *, tm=256):
    t = x2.shape[0]
    const = pl.Buffered(1)
    base = 4 * A_WIDTH // B_WIDTH
    vec = lambda n: pl.BlockSpec((1, n), lambda i: (0, 0))
    return pl.pallas_call(
        functools.partial(_out0_kernel, rows=tm),
        grid=(t // tm,),
        in_specs=[
            pl.BlockSpec((tm, A_WIDTH), lambda i: (i, 0)),
            pl.BlockSpec((tm, B_WIDTH), lambda i: (i, base)),
            pl.BlockSpec((tm, B_WIDTH), lambda i: (i, base + 1)),
            pl.BlockSpec((tm, B_WIDTH), lambda i: (i, base + 2)),
            pl.BlockSpec((tm, D_MODEL), lambda i: (i, 0)),
            pl.BlockSpec((A_WIDTH, D_MODEL), lambda i: (0, 0), pipeline_mode=const),
            pl.BlockSpec((B_WIDTH, D_MODEL), lambda i: (1, 0), pipeline_mode=const),
            vec(D_MODEL), vec(B_WIDTH), vec(B_WIDTH),
            pl.BlockSpec((B_GROUPS, B_SPAN, B_SPAN), lambda i: (0, 0, 0)),
            pl.BlockSpec((B_SPAN, B_GROUPS), lambda i: (0, 0)),
        ],
        out_specs=[
            pl.BlockSpec((tm, D_MODEL), lambda i: (i, 0)),
            pl.BlockSpec((tm, D_MODEL), lambda i: (i, 0)),
        ],
        out_shape=[
            jax.ShapeDtypeStruct((t, D_MODEL), jnp.float32),
            jax.ShapeDtypeStruct((t, D_MODEL), jnp.bfloat16),
        ],
        scratch_shapes=[pltpu.VMEM((tm, B_WIDTH), jnp.bfloat16)],
        compiler_params=pltpu.CompilerParams(
            dimension_semantics=("arbitrary",),
            vmem_limit_bytes=VMEM_LIMIT),
        name="out0",
    )(ya, proj, proj, proj, x2, w_bf, w_bf, g, ln_g, ln_b, w_s, b_st)


def _proj1_kernel(xn_ref, w_ref, o_ref):
    def proj():
        return jnp.dot(xn_ref[...], w_ref[...], preferred_element_type=jnp.float32)

    @pl.when(pl.program_id(0) == 0)
    def _():
        o_ref[...] = proj().astype(o_ref.dtype)

    @pl.when(pl.program_id(0) == 1)
    def _():
        o_ref[...] = _silu(proj()).astype(o_ref.dtype)


def _proj1(xn, w_bf, *, tm=512):
    t = xn.shape[0]
    return pl.pallas_call(
        _proj1_kernel,
        grid=(2, t // tm),
        in_specs=[
            pl.BlockSpec((tm, D_MODEL), lambda j, i: (i, 0)),
            pl.BlockSpec((D_MODEL, C_WIDTH), lambda j, i: (0, j)),
        ],
        out_specs=pl.BlockSpec((tm, C_WIDTH), lambda j, i: (i, j)),
        out_shape=jax.ShapeDtypeStruct((t, 2 * C_WIDTH), jnp.bfloat16),
        compiler_params=pltpu.CompilerParams(
            dimension_semantics=("arbitrary", "arbitrary"),
            vmem_limit_bytes=VMEM_LIMIT),
        name="proj1",
    )(xn, w_bf)


def _segment_pitch(seg):
    units = seg // SUBLANES
    return SUBLANES * (units + 1 if units % 2 == 0 else units)


def _build_gate_slabs(wp_ref, wg_sc, tmp_sc):
    row = lax.broadcasted_iota(jnp.int32, (GATE_PAD, LANES), 0)
    col = lax.broadcasted_iota(jnp.int32, (GATE_PAD, LANES), 1)
    for j in range(N_CT):
        tmp_sc[...] = jnp.zeros(tmp_sc.shape, jnp.float32)
        h_lo = (LANES * j) // C_BLOCK
        h_hi = (LANES * j + LANES - 1) // C_BLOCK
        for gate in range(2):
            for h in range(h_lo, h_hi + 1):
                sel = jnp.where(row == col + (LANES * j - C_BLOCK * h), 1.0, 0.0)
                part = jnp.dot(wp_ref[gate, h], sel.astype(jnp.bfloat16),
                               preferred_element_type=jnp.float32)
                r0 = C_BLOCK * h - GATE_WS[j]
                cs = slice(gate * LANES, (gate + 1) * LANES)
                tmp_sc[r0:r0 + C_BLOCK, cs] = tmp_sc[r0:r0 + C_BLOCK, cs] + part[:C_BLOCK, :]
        wg_sc[j] = (0.5 * tmp_sc[...]).astype(jnp.bfloat16)


def _rglru_kernel(xb_ref, zc_ref, cw_ref, cb_ref, wp_ref, ba_ref, bx_ref,
                  lam_ref, o_ref, pad_sc, xc_sc, xcb_sc, ga_sc, gx_sc, h_sc,
                  wg_ref, tmp_sc, hl_sc, p_sc, prm_sc, *, ts, rows):
    si = pl.program_id(1)
    halo = SUBLANES
    vshape = (N_CT, SUBLANES, LANES)

    @pl.when((pl.program_id(0) == 0) & (si == 0))
    def _():
        _build_gate_slabs(wp_ref, wg_ref, tmp_sc)
        lam = lam_ref[...]
        sp = jnp.maximum(-lam, 0.0) + jnp.log1p(jnp.exp(-jnp.abs(lam)))
        prm_sc[0] = jnp.broadcast_to((-0.5 * C_GATE_C) * sp, vshape)
        prm_sc[1] = jnp.broadcast_to(0.5 * ba_ref[...], vshape)
        prm_sc[2] = jnp.broadcast_to(0.5 * bx_ref[...], vshape)

    @pl.when(si == 0)
    def _():
        pad_sc[0:halo, :] = jnp.zeros((halo, C_WIDTH), jnp.float32)
        h_sc[...] = jnp.zeros(h_sc.shape, jnp.float32)

    @pl.when(si > 0)
    def _():
        pad_sc[0:halo, :] = pad_sc[ts:ts + halo, :]

    pad_sc[halo:halo + ts, :] = xb_ref[...].astype(jnp.float32)
    xc = cb_ref[...] + cw_ref[C_CONV - 1:C_CONV, :] * pad_sc[halo:halo + ts, :]
    for tap in range(C_CONV - 1):
        off = halo - (C_CONV - 1) + tap
        xc = xc + cw_ref[tap:tap + 1, :] * pad_sc[off:off + ts, :]
    xcb_sc[...] = xc.astype(jnp.bfloat16)
    seg = ts // SUBLANES
    pitch = _segment_pitch(seg)

    def put(dst_sc, j, val):
        for s in range(SUBLANES):
            dst_sc[j, s * pitch:s * pitch + seg, :] = val[s * seg:(s + 1) * seg, :]

    for j in range(N_CT):
        put(xc_sc, j, xc[:, j * LANES:(j + 1) * LANES])

    for j in range(N_CT):
        ws = GATE_WS[j]
        g = jnp.dot(xcb_sc[:, ws:ws + GATE_WIN], wg_ref[j],
                    preferred_element_type=jnp.float32)
        put(ga_sc, j, g[:, :LANES])
        put(gx_sc, j, g[:, LANES:])

    blocks_per_seg = seg // rows

    def padded_rows(it):
        start = (it // blocks_per_seg) * pitch + (it % blocks_per_seg) * rows
        return pl.ds(pl.multiple_of(start, SUBLANES), rows)

    def local(i, carry):
        h, p = carry
        rs = pl.ds(i, SUBLANES, stride=pitch)
        half_c = prm_sc[0]
        tr = jnp.tanh(ga_sc[:, rs, :] + prm_sc[1])
        ti = jnp.tanh(gx_sc[:, rs, :] + prm_sc[2])
        log_a = half_c + half_c * tr
        a = jnp.exp(log_a)
        half_a = 0.5 * a
        y = (-0.25 - half_a * half_a) * jnp.tanh(log_a)
        half_beta = jnp.where(y > 0.0, y * lax.rsqrt(y), 0.0)
        x = xc_sc[:, rs, :]
        b = half_beta * (x + x * ti)
        h = a * h + b
        p = a * p
        blk = pl.ds(pl.multiple_of(i * SUBLANES, SUBLANES), SUBLANES)
        hl_sc[:, blk, :] = h
        p_sc[:, blk, :] = p
        return h, p

    h_end, p_end = lax.fori_loop(
        0, seg, local,
        (jnp.zeros(vshape, jnp.float32), jnp.ones(vshape, jnp.float32)), unroll=2)

    sub = lax.broadcasted_iota(jnp.int32, vshape, 1)
    a, b = p_end, h_end
    for d in (1, 2, 4):
        keep = sub >= d
        a_sh = pltpu.roll(a, d, 1)
        b_sh = pltpu.roll(b, d, 1)
        b = jnp.where(keep, a * b_sh + b, b)
        a = jnp.where(keep, a * a_sh, a)
    carry_in = h_sc[...]
    seg_end = a * carry_in + b
    h_in = jnp.where(sub == 0, carry_in, pltpu.roll(seg_end, 1, 1))
    h_sc[...] = jnp.broadcast_to(seg_end[:, SUBLANES - 1:SUBLANES, :], vshape)

    def fix(i, c):
        blk = pl.ds(pl.multiple_of(i * SUBLANES, SUBLANES), SUBLANES)
        xc_sc[:, pl.ds(i, SUBLANES, stride=pitch), :] = hl_sc[:, blk, :] + p_sc[:, blk, :] * h_in
        return c

    lax.fori_loop(0, seg, fix, 0)

    def emit(it, c):
        rs = pl.ds(pl.multiple_of(it * rows, rows), rows)
        ps = padded_rows(it)
        h = jnp.concatenate([xc_sc[j, ps, :] for j in range(N_CT)], axis=-1)
        o_ref[rs, :] = (h * zc_ref[rs, :].astype(jnp.float32)).astype(o_ref.dtype)
        return c

    lax.fori_loop(0, ts // rows, emit, 0)


def _rglru(xbz3, cw, cb, wp, ba, bx, lam, *, ts=256, rows=16):
    b, s, _ = xbz3.shape
    const = pl.Buffered(1)
    vec = lambda: pl.BlockSpec((N_CT, 1, LANES), lambda bi, si: (0, 0, 0))
    tiled = lambda: pltpu.VMEM((N_CT, ts, LANES), jnp.float32)
    padded = lambda: pltpu.VMEM(
        (N_CT, SUBLANES * _segment_pitch(ts // SUBLANES), LANES), jnp.float32)
    tile_major = lambda v: v.reshape(N_CT, 1, LANES)
    return pl.pallas_call(
        functools.partial(_rglru_kernel, ts=ts, rows=rows),
        grid=(b, s // ts),
        in_specs=[
            pl.BlockSpec((None, ts, C_WIDTH), lambda bi, si: (bi, si, 0)),
            pl.BlockSpec((None, ts, C_WIDTH), lambda bi, si: (bi, si, 1)),
            pl.BlockSpec((C_CONV, C_WIDTH), lambda bi, si: (0, 0)),
            pl.BlockSpec((1, C_WIDTH), lambda bi, si: (0, 0)),
            pl.BlockSpec((2, C_HEADS, GATE_PAD, GATE_PAD), lambda bi, si: (0, 0, 0, 0),
                         pipeline_mode=const),
            vec(), vec(), vec(),
        ],
        out_specs=pl.BlockSpec((None, ts, C_WIDTH), lambda bi, si: (bi, si, 0)),
        out_shape=jax.ShapeDtypeStruct((b, s, C_WIDTH), jnp.bfloat16),
        scratch_shapes=[
            pltpu.VMEM((ts + SUBLANES, C_WIDTH), jnp.float32),
            padded(),
            pltpu.VMEM((ts, C_WIDTH), jnp.bfloat16),
            padded(),
            padded(),
            pltpu.VMEM((N_CT, SUBLANES, LANES), jnp.float32),
            pltpu.VMEM((N_CT, GATE_WIN, 2 * LANES), jnp.bfloat16),
            pltpu.VMEM((GATE_WIN, 2 * LANES), jnp.float32),
            tiled(),
            tiled(),
            pltpu.VMEM((3, N_CT, SUBLANES, LANES), jnp.float32),
        ],
        compiler_params=pltpu.CompilerParams(
            dimension_semantics=("arbitrary", "arbitrary"),
            vmem_limit_bytes=VMEM_LIMIT),
        name="rglru",
    )(xbz3, xbz3, cw, cb, wp, tile_major(ba), tile_major(bx), tile_major(lam))


def _out1_kernel(hz_ref, x_ref, w_ref, g_ref, o_ref):
    y = jnp.dot(hz_ref[...], w_ref[...], preferred_element_type=jnp.float32)
    o_ref[...] = _rms_rows(x_ref[...] + y, g_ref[...])


def _out1(hz, x1, w_bf, g, *, tm=256):
    t = x1.shape[0]
    return pl.pallas_call(
        _out1_kernel,
        grid=(t // tm,),
        in_specs=[
            pl.BlockSpec((tm, C_WIDTH), lambda i: (i, 0)),
            pl.BlockSpec((tm, D_MODEL), lambda i: (i, 0)),
            pl.BlockSpec((C_WIDTH, D_MODEL), lambda i: (0, 0),
                         pipeline_mode=pl.Buffered(1)),
            pl.BlockSpec((1, D_MODEL), lambda i: (0, 0)),
        ],
        out_specs=pl.BlockSpec((tm, D_MODEL), lambda i: (i, 0)),
        out_shape=jax.ShapeDtypeStruct((t, D_MODEL), jnp.float32),
        compiler_params=pltpu.CompilerParams(
            dimension_semantics=("arbitrary",),
            vmem_limit_bytes=VMEM_LIMIT),
        name="out1",
    )(hz, x1, w_bf, g)


def kernel(x, ab_norm, ab_w_in, ab_lambda, ab_head_norm, ab_sgu_ln_g, ab_sgu_ln_b, ab_sgu_w, ab_sgu_b, ab_w_out, c_norm, c_w_in, c_conv_w, c_conv_b, c_gate_a_w, c_gate_a_b, c_gate_x_w, c_gate_x_b, c_lambda, c_w_out, final_norm):
    bsz, seq, d = x.shape
    t = bsz * seq
    bf = jnp.bfloat16
    row = lambda v: v.reshape(1, -1)
    x2 = x.reshape(t, d)

    proj, (w_out0, w_in1, w_out1) = _proj0(
        x2, row(ab_norm[0]), ab_w_in[0], [ab_w_out[0], c_w_in[0], c_w_out[0]])
    ya = _attention(proj.reshape(bsz, seq, AB_IN), ab_lambda[0], row(ab_head_norm[0]))
    x1, xn1 = _out0(ya.reshape(t, A_WIDTH), proj, x2, w_out0, row(c_norm[0]),
                    row(ab_sgu_ln_g[0]), row(ab_sgu_ln_b[0]), ab_sgu_w[0], ab_sgu_b[0].T)

    xbz = _proj1(xn1, w_in1)
    padw = GATE_PAD - C_BLOCK
    wp = jnp.pad(jnp.stack([c_gate_a_w[0], c_gate_x_w[0]]).astype(bf),
                 ((0, 0), (0, 0), (0, padw), (0, padw)))
    hz = _rglru(xbz.reshape(bsz, seq, 2 * C_WIDTH), c_conv_w[0], row(c_conv_b[0]), wp,
                row(c_gate_a_b[0]), row(c_gate_x_b[0]), row(c_lambda[0]))
    out = _out1(hz.reshape(t, C_WIDTH), x1, w_out1, row(final_norm))
    return out.reshape(bsz, seq, d)
```

```python
import functools
import math

import jax
import jax.numpy as jnp
import numpy as np
from jax import lax
from jax.experimental import pallas as pl
from jax.experimental.pallas import tpu as pltpu

D_MODEL = 2048
CHUNK = 64
CHUNK_SHIFT = 6
NORM_EPS = 1e-6
A_HEADS = 8
A_HEAD_DIM = 128
A_VDIM = 2 * A_HEAD_DIM
A_WIDTH = A_HEADS * A_VDIM
B_GROUPS = 8
B_WIDTH = D_MODEL
B_GROUP_DIM = B_WIDTH // B_GROUPS
B_SPAN = 128
C_WIDTH = 2688
C_HEADS = 16
C_BLOCK = C_WIDTH // C_HEADS
C_CONV = 4
C_GATE_C = 8.0
AB_IN = 4 * A_WIDTH + 3 * B_WIDTH
LAM_INIT_0 = 0.8 - 0.6 * math.exp(-0.3 * 0)

LANES = 128
SUBLANES = 8
LOG2E = 1.4426950408889634
VMEM_LIMIT = 56 * 1024 * 1024
SIDE_CAST_MAX_STEPS = 64
BF16_ROWS = 16
GATE_WIN = 512
GATE_PAD = 256
N_CT = C_WIDTH // LANES


def _gate_window_start(j):
    h0 = (LANES * j) // C_BLOCK
    ws = (C_BLOCK * h0) // LANES * LANES
    return min(ws, C_WIDTH - GATE_WIN)


GATE_WS = tuple(_gate_window_start(j) for j in range(N_CT))


def _silu(z):
    h = 0.5 * z
    return h + h * jnp.tanh(h)


def _gelu_tanh(x):
    c = math.sqrt(2.0 / math.pi)
    h = 0.5 * x
    return h + h * jnp.tanh(x * (c + (c * 0.044715) * (x * x)))


def _rms_rows(xf, g):
    ms = jnp.mean(xf * xf, axis=-1, keepdims=True)
    return xf * lax.rsqrt(ms + NORM_EPS) * g


def _proj0_kernel(x_ref, g_ref, w_ref, *rest, tiles_per_seg, emit_weights, side_steps):
    j = pl.program_id(1)
    if emit_weights:
        o_ref, wb_ref, xn_sc = rest
        wb_ref[...] = w_ref[...].astype(wb_ref.dtype)
    else:
        n_side = len(side_steps)
        side_in, o_ref = rest[:n_side], rest[n_side + 1]
        side_out, xn_sc = rest[n_side + 2:2 * n_side + 2], rest[-1]
        step = pl.program_id(0) * pl.num_programs(1) + j
        for src, dst, steps in zip(side_in, side_out, side_steps):
            @pl.when(step < steps)
            def _(src=src, dst=dst):
                dst[...] = src[...].astype(dst.dtype)

    @pl.when(j == 0)
    def _():
        xn_sc[...] = _rms_rows(x_ref[...], g_ref[...]).astype(xn_sc.dtype)

    seg = j // tiles_per_seg

    def proj():
        return jnp.dot(xn_sc[...], w_ref[...], preferred_element_type=jnp.float32)

    @pl.when(seg <= 2)
    def _():
        scale = jnp.where(seg == 0, A_HEAD_DIM ** -0.5 * LOG2E, 1.0).astype(jnp.float32)
        o_ref[...] = (proj() * scale).astype(o_ref.dtype)

    @pl.when((seg == 3) | (seg == 6))
    def _():
        o_ref[...] = _silu(proj()).astype(o_ref.dtype)

    @pl.when((seg == 4) | (seg == 5))
    def _():
        o_ref[...] = _gelu_tanh(proj()).astype(o_ref.dtype)


def _side_cast_steps(rows):
    return max(s for s in range(1, SIDE_CAST_MAX_STEPS + 1) if rows % (s * BF16_ROWS) == 0)


def _proj0(x2, g, w, later_weights, *, tm=1024, tn=1024):
    t = x2.shape[0]
    n = w.shape[1]
    grid_body = (t // tm - 1, n // tn)
    side = list(later_weights)
    side_steps = tuple(_side_cast_steps(a.shape[0]) for a in side)
    assert grid_body[0] * grid_body[1] >= max(side_steps)
    side_specs = [
        pl.BlockSpec((a.shape[0] // steps, a.shape[1]),
                     lambda i, j, steps=steps: (jnp.minimum(i * grid_body[1] + j, steps - 1), 0))
        for a, steps in zip(side, side_steps)]
    params = pltpu.CompilerParams(
        dimension_semantics=("arbitrary", "arbitrary"), vmem_limit_bytes=VMEM_LIMIT)
    x_spec = lambda first: pl.BlockSpec((tm, D_MODEL), lambda i, j: (i + first, 0))
    g_spec = pl.BlockSpec((1, D_MODEL), lambda i, j: (0, 0))
    w_spec = pl.BlockSpec((D_MODEL, tn), lambda i, j: (0, j))
    o_spec = lambda first: pl.BlockSpec((tm, tn), lambda i, j: (i + first, j))
    proj_shape = jax.ShapeDtypeStruct((t, n), jnp.bfloat16)
    proj, w_bf = pl.pallas_call(
        functools.partial(_proj0_kernel, tiles_per_seg=A_WIDTH // tn, emit_weights=True,
                          side_steps=()),
        grid=(1, n // tn),
        in_specs=[x_spec(0), g_spec, w_spec],
        out_specs=[o_spec(0), w_spec],
        out_shape=[proj_shape, jax.ShapeDtypeStruct(w.shape, jnp.bfloat16)],
        scratch_shapes=[pltpu.VMEM((tm, D_MODEL), w.dtype)],
        compiler_params=params,
        name="proj0_head",
    )(x2, g, w)
    outs = pl.pallas_call(
        functools.partial(_proj0_kernel, tiles_per_seg=A_WIDTH // tn, emit_weights=False,
                          side_steps=side_steps),
        grid=grid_body,
        in_specs=[x_spec(1), g_spec, w_spec] + side_specs
                 + [pl.BlockSpec(memory_space=pl.ANY)],
        out_specs=[o_spec(1)] + side_specs,
        out_shape=[proj_shape] + [jax.ShapeDtypeStruct(a.shape, jnp.bfloat16) for a in side],
        input_output_aliases={3 + len(side): 0},
        scratch_shapes=[pltpu.VMEM((tm, D_MODEL), jnp.bfloat16)],
        compiler_params=params,
        name="proj0",
    )(x2, g, w_bf, *side, proj)
    return outs[0], outs[1:]


ATT_WIDE = 1


def _attn_schedule(nq):
    wide = [(i, j, ATT_WIDE, False)
            for j in range(0, nq, ATT_WIDE) for i in range(j + ATT_WIDE, nq)]
    narrow = [(i, j, 1, False)
              for i in range(nq) for j in range(i - i % ATT_WIDE, i)]
    diag = [(i, i, 1, True) for i in range(nq)]
    steps = wide + narrow + diag
    assert len(steps) % 2 == 0
    first_visit = {}
    for n, st in enumerate(steps):
        first_visit.setdefault(st[0], n)
    flags = []
    for n, (i, _, w, masked) in enumerate(steps):
        nxt = steps[n + 1][2] if n + 1 < len(steps) else 0
        flags.append((w, masked, first_visit[i] == n, nxt))
    pairs = [(flags[n], flags[n + 1]) for n in range(0, len(steps), 2)]
    runs = []
    for k, pf in enumerate(pairs):
        if runs and runs[-1][2] == pf:
            runs[-1] = (runs[-1][0], runs[-1][1] + 1, pf)
        else:
            runs.append((k, 1, pf))
    return steps, tuple(runs)


def _attn_kernel(it_ref, jt_ref, lamp_ref, q_ref, k_ref, v_ref, za_ref, hn_ref, o_ref,
                 m_sc, l_sc, acc_sc, s0_sc, s1_sc, *, t, pair_runs):
    s_bufs = (s0_sc, s1_sc)
    nt = (((1,), (1,)), ((), ()))

    lp = lamp_ref[...]
    d1 = jnp.sum(lp[0:1, :] * lp[1:2, :], axis=1, keepdims=True)
    d2 = jnp.sum(lp[2:3, :] * lp[3:4, :], axis=1, keepdims=True)
    lam = jnp.exp(d1) - jnp.exp(d2) + LAM_INIT_0

    def rows_of(tile, tiles=1):
        return pl.ds(pl.multiple_of(tile * t, t), tiles * t)

    def scores(n, width, s_ref):
        qs, ks = rows_of(it_ref[n]), rows_of(jt_ref[n], width)
        s_ref[0:t, 0:width * t] = lax.dot_general(
            q_ref[qs, :A_HEAD_DIM], k_ref[ks, :A_HEAD_DIM], nt,
            preferred_element_type=jnp.float32)
        s_ref[t:2 * t, 0:width * t] = lax.dot_general(
            q_ref[qs, A_HEAD_DIM:], k_ref[ks, A_HEAD_DIM:], nt,
            preferred_element_type=jnp.float32)

    def softmax_pv(n, s_ref, width, masked, first):
        i = it_ref[n]
        keys = width * t
        s = s_ref[:, 0:keys]
        if masked:
            rq = lax.broadcasted_iota(jnp.int32, (2 * t, 1), 0)
            rq = lax.shift_right_logical(jnp.where(rq >= t, rq - t, rq), CHUNK_SHIFT)
            ck = lax.shift_right_logical(
                lax.broadcasted_iota(jnp.int32, (1, keys), 1), CHUNK_SHIFT)
            s = jnp.where(ck <= rq, s, -jnp.inf)
        v = v_ref[rows_of(jt_ref[n], width), :]
        if first:
            m_new = jnp.broadcast_to(jnp.max(s, axis=1, keepdims=True), (2 * t, LANES))
            p = jnp.exp2(s - pltpu.repeat(m_new, keys // LANES, 1))
            l_sc[i] = jnp.broadcast_to(jnp.sum(p, axis=1, keepdims=True), (2 * t, LANES))
            m_sc[i] = m_new
            acc_sc[i] = jnp.dot(p.astype(jnp.bfloat16), v, preferred_element_type=jnp.float32)
            return
        m_prev = m_sc[i]
        m_new = jnp.maximum(m_prev, jnp.max(s, axis=1, keepdims=True))
        alpha = jnp.exp2(m_prev - m_new)
        p = jnp.exp2(s - pltpu.repeat(m_new, keys // LANES, 1))
        l_sc[i] = alpha * l_sc[i] + jnp.sum(p, axis=1, keepdims=True)
        m_sc[i] = m_new
        pv = jnp.dot(p.astype(jnp.bfloat16), v,
                     preferred_element_type=jnp.float32)
        acc_sc[i] = acc_sc[i] * pltpu.repeat(alpha, A_VDIM // LANES, 1) + pv

    def finish(i):
        inv_l = 1.0 / l_sc[i]
        acc = acc_sc[i] * pltpu.repeat(inv_l, A_VDIM // LANES, 1)
        o = acc[:t, :] - lam * acc[t:, :]
        ms = jnp.mean(o * o, axis=-1, keepdims=True)
        o = o * lax.rsqrt(ms + NORM_EPS) * hn_ref[...] * (1.0 - LAM_INIT_0)
        rows = rows_of(i)
        o_ref[rows, :] = (o * za_ref[rows, :].astype(jnp.float32)).astype(o_ref.dtype)

    def tick(n, par, width, masked, first, next_width):
        if next_width:
            scores(n + 1, next_width, s_bufs[1 - par])
        softmax_pv(n, s_bufs[par], width, masked, first)
        if masked:
            finish(it_ref[n])

    scores(0, pair_runs[0][2][0][0], s0_sc)
    for start, count, flags in pair_runs:
        def pair(pp, c, start=start, flags=flags):
            n = 2 * (start + pp)
            tick(n, 0, *flags[0])
            tick(n + 1, 1, *flags[1])
            return c
        if count == 1:
            pair(0, 0)
        else:
            lax.fori_loop(0, count, pair, 0)


def _attention(proj3, lamp, hn, *, t=512):
    b, s, _ = proj3.shape
    nblk = A_WIDTH // A_VDIM
    nq = s // t
    steps, pair_runs = _attn_schedule(nq)
    it = jnp.asarray([st[0] for st in steps], jnp.int32)
    jt = jnp.asarray([st[1] for st in steps], jnp.int32)
    head = lambda blk: pl.BlockSpec((None, s, A_VDIM),
                                    lambda bi, h, it_r, jt_r: (bi, 0, blk * nblk + h))
    grid_spec = pltpu.PrefetchScalarGridSpec(
        num_scalar_prefetch=2,
        grid=(b, A_HEADS),
        in_specs=[
            pl.BlockSpec((4, A_HEAD_DIM), lambda bi, h, it_r, jt_r: (0, 0)),
            head(0), head(1), head(2), head(3),
            pl.BlockSpec((1, A_VDIM), lambda bi, h, it_r, jt_r: (0, h)),
        ],
        out_specs=pl.BlockSpec((None, s, A_VDIM), lambda bi, h, it_r, jt_r: (bi, 0, h)),
        scratch_shapes=[
            pltpu.VMEM((nq, 2 * t, LANES), jnp.float32),
            pltpu.VMEM((nq, 2 * t, LANES), jnp.float32),
            pltpu.VMEM((nq, 2 * t, A_VDIM), jnp.float32),
            pltpu.VMEM((2 * t, ATT_WIDE * t), jnp.float32),
            pltpu.VMEM((2 * t, ATT_WIDE * t), jnp.float32),
        ],
    )
    return pl.pallas_call(
        functools.partial(_attn_kernel, t=t, pair_runs=pair_runs),
        grid_spec=grid_spec,
        out_shape=jax.ShapeDtypeStruct((b, s, A_WIDTH), jnp.bfloat16),
        compiler_params=pltpu.CompilerParams(
            dimension_semantics=("arbitrary", "arbitrary"),
            vmem_limit_bytes=VMEM_LIMIT),
        name="diff_attn",
    )(it, jt, lamp, proj3, proj3, proj3, proj3, hn)


def _sgu_tile(gu_ref, gv_ref, zb_ref, lng_ref, lnb_ref, ws_ref, bst_ref, yb_sc, rows):
    v = gv_ref[...].astype(jnp.float32)
    mu = jnp.mean(v, axis=-1, keepdims=True)
    vc = v - mu
    var = jnp.mean(vc * vc, axis=-1, keepdims=True)
    vln = (vc * lax.rsqrt(var + NORM_EPS) * lng_ref[...] + lnb_ref[...])
    vln = vln.astype(jnp.bfloat16)
    ci = lax.broadcasted_iota(jnp.int32, (B_SPAN, B_SPAN), 0) // CHUNK
    cj = lax.broadcasted_iota(jnp.int32, (B_SPAN, B_SPAN), 1) // CHUNK
    keep = ci >= cj
    for g in range(B_GROUPS):
        w = jnp.where(keep, ws_ref[g], 0.0).astype(jnp.bfloat16)
        bcol = bst_ref[:, g:g + 1]
        c0 = g * B_GROUP_DIM
        for sp in range(rows // B_SPAN):
            r0 = sp * B_SPAN
            mixed = jnp.dot(w, vln[r0:r0 + B_SPAN, c0:c0 + B_GROUP_DIM],
                            preferred_element_type=jnp.float32) + bcol
            u = gu_ref[r0:r0 + B_SPAN, c0:c0 + B_GROUP_DIM].astype(jnp.float32)
            z = zb_ref[r0:r0 + B_SPAN, c0:c0 + B_GROUP_DIM].astype(jnp.float32)
            yb_sc[r0:r0 + B_SPAN, c0:c0 + B_GROUP_DIM] = (u * mixed * z).astype(yb_sc.dtype)


def _out0_kernel(ya_ref, gu_ref, gv_ref, zb_ref, x_ref, wa_ref, wb_ref, g_ref,
                 lng_ref, lnb_ref, ws_ref, bst_ref, x1_ref, xn_ref, yb_sc, *, rows):
    y = jnp.dot(ya_ref[...], wa_ref[...], preferred_element_type=jnp.float32)
    _sgu_tile(gu_ref, gv_ref, zb_ref, lng_ref, lnb_ref, ws_ref, bst_ref, yb_sc, rows)
    y = y + jnp.dot(yb_sc[...], wb_ref[...], preferred_element_type=jnp.float32)
    x1 = x_ref[...] + y
    x1_ref[...] = x1
    xn_ref[...] = _rms_rows(x1, g_ref[...]).astype(xn_ref.dtype)


def _out0(ya, proj, x2, w_bf, g, ln_g, ln_b, w_s, b_st, *, tm=256):
    t = x2.shape[0]
    const = pl.Buffered(1)
    base = 4 * A_WIDTH // B_WIDTH
    vec = lambda n: pl.BlockSpec((1, n), lambda i: (0, 0))
    return pl.pallas_call(
        functools.partial(_out0_kernel, rows=tm),
        grid=(t // tm,),
        in_specs=[
            pl.BlockSpec((tm, A_WIDTH), lambda i: (i, 0)),
            pl.BlockSpec((tm, B_WIDTH), lambda i: (i, base)),
            pl.BlockSpec((tm, B_WIDTH), lambda i: (i, base + 1)),
            pl.BlockSpec((tm, B_WIDTH), lambda i: (i, base + 2)),
            pl.BlockSpec((tm, D_MODEL), lambda i: (i, 0)),
            pl.BlockSpec((A_WIDTH, D_MODEL), lambda i: (0, 0), pipeline_mode=const),
            pl.BlockSpec((B_WIDTH, D_MODEL), lambda i: (1, 0), pipeline_mode=const),
            vec(D_MODEL), vec(B_WIDTH), vec(B_WIDTH),
            pl.BlockSpec((B_GROUPS, B_SPAN, B_SPAN), lambda i: (0, 0, 0)),
            pl.BlockSpec((B_SPAN, B_GROUPS), lambda i: (0, 0)),
        ],
        out_specs=[
            pl.BlockSpec((tm, D_MODEL), lambda i: (i, 0)),
            pl.BlockSpec((tm, D_MODEL), lambda i: (i, 0)),
        ],
        out_shape=[
            jax.ShapeDtypeStruct((t, D_MODEL), jnp.float32),
            jax.ShapeDtypeStruct((t, D_MODEL), jnp.bfloat16),
        ],
        scratch_shapes=[pltpu.VMEM((tm, B_WIDTH), jnp.bfloat16)],
        compiler_params=pltpu.CompilerParams(
            dimension_semantics=("arbitrary",),
            vmem_limit_bytes=VMEM_LIMIT),
        name="out0",
    )(ya, proj, proj, proj, x2, w_bf, w_bf, g, ln_g, ln_b, w_s, b_st)


def _proj1_kernel(xn_ref, w_ref, o_ref):
    def proj():
        return jnp.dot(xn_ref[...], w_ref[...], preferred_element_type=jnp.float32)

    @pl.when(pl.program_id(0) == 0)
    def _():
        o_ref[...] = proj().astype(o_ref.dtype)

    @pl.when(pl.program_id(0) == 1)
    def _():
        o_ref[...] = _silu(proj()).astype(o_ref.dtype)


def _proj1(xn, w_bf, *, tm=512):
    t = xn.shape[0]
    return pl.pallas_call(
        _proj1_kernel,
        grid=(2, t // tm),
        in_specs=[
            pl.BlockSpec((tm, D_MODEL), lambda j, i: (i, 0)),
            pl.BlockSpec((D_MODEL, C_WIDTH), lambda j, i: (0, j)),
        ],
        out_specs=pl.BlockSpec((tm, C_WIDTH), lambda j, i: (i, j)),
        out_shape=jax.ShapeDtypeStruct((t, 2 * C_WIDTH), jnp.bfloat16),
        compiler_params=pltpu.CompilerParams(
            dimension_semantics=("arbitrary", "arbitrary"),
            vmem_limit_bytes=VMEM_LIMIT),
        name="proj1",
    )(xn, w_bf)


def _segment_pitch(seg):
    units = seg // SUBLANES
    return SUBLANES * (units + 1 if units % 2 == 0 else units)


def _build_gate_slabs(wp_ref, wg_sc, tmp_sc):
    row = lax.broadcasted_iota(jnp.int32, (GATE_PAD, LANES), 0)
    col = lax.broadcasted_iota(jnp.int32, (GATE_PAD, LANES), 1)
    for j in range(N_CT):
        tmp_sc[...] = jnp.zeros(tmp_sc.shape, jnp.float32)
        h_lo = (LANES * j) // C_BLOCK
        h_hi = (LANES * j + LANES - 1) // C_BLOCK
        for gate in range(2):
            for h in range(h_lo, h_hi + 1):
                sel = jnp.where(row == col + (LANES * j - C_BLOCK * h), 1.0, 0.0)
                part = jnp.dot(wp_ref[gate, h], sel.astype(jnp.bfloat16),
                               preferred_element_type=jnp.float32)
                r0 = C_BLOCK * h - GATE_WS[j]
                cs = slice(gate * LANES, (gate + 1) * LANES)
                tmp_sc[r0:r0 + C_BLOCK, cs] = tmp_sc[r0:r0 + C_BLOCK, cs] + part[:C_BLOCK, :]
        wg_sc[j] = (0.5 * tmp_sc[...]).astype(jnp.bfloat16)


def _rglru_kernel(xb_ref, zc_ref, cw_ref, cb_ref, wp_ref, ba_ref, bx_ref,
                  lam_ref, o_ref, pad_sc, xc_sc, xcb_sc, ga_sc, gx_sc, h_sc,
                  wg_ref, tmp_sc, hl_sc, p_sc, prm_sc, *, ts, rows):
    si = pl.program_id(1)
    halo = SUBLANES
    vshape = (N_CT, SUBLANES, LANES)

    @pl.when((pl.program_id(0) == 0) & (si == 0))
    def _():
        _build_gate_slabs(wp_ref, wg_ref, tmp_sc)
        lam = lam_ref[...]
        sp = jnp.maximum(-lam, 0.0) + jnp.log1p(jnp.exp(-jnp.abs(lam)))
        prm_sc[0] = jnp.broadcast_to((-0.5 * C_GATE_C) * sp, vshape)
        prm_sc[1] = jnp.broadcast_to(0.5 * ba_ref[...], vshape)
        prm_sc[2] = jnp.broadcast_to(0.5 * bx_ref[...], vshape)

    @pl.when(si == 0)
    def _():
        pad_sc[0:halo, :] = jnp.zeros((halo, C_WIDTH), jnp.float32)
        h_sc[...] = jnp.zeros(h_sc.shape, jnp.float32)

    @pl.when(si > 0)
    def _():
        pad_sc[0:halo, :] = pad_sc[ts:ts + halo, :]

    pad_sc[halo:halo + ts, :] = xb_ref[...].astype(jnp.float32)
    xc = cb_ref[...] + cw_ref[C_CONV - 1:C_CONV, :] * pad_sc[halo:halo + ts, :]
    for tap in range(C_CONV - 1):
        off = halo - (C_CONV - 1) + tap
        xc = xc + cw_ref[tap:tap + 1, :] * pad_sc[off:off + ts, :]
    xcb_sc[...] = xc.astype(jnp.bfloat16)
    seg = ts // SUBLANES
    pitch = _segment_pitch(seg)

    def put(dst_sc, j, val):
        for s in range(SUBLANES):
            dst_sc[j, s * pitch:s * pitch + seg, :] = val[s * seg:(s + 1) * seg, :]

    for j in range(N_CT):
        put(xc_sc, j, xc[:, j * LANES:(j + 1) * LANES])

    for j in range(N_CT):
        ws = GATE_WS[j]
        g = jnp.dot(xcb_sc[:, ws:ws + GATE_WIN], wg_ref[j],
                    preferred_element_type=jnp.float32)
        put(ga_sc, j, g[:, :LANES])
        put(gx_sc, j, g[:, LANES:])

    blocks_per_seg = seg // rows

    def padded_rows(it):
        start = (it // blocks_per_seg) * pitch + (it % blocks_per_seg) * rows
        return pl.ds(pl.multiple_of(start, SUBLANES), rows)

    def local(i, carry):
        h, p = carry
        rs = pl.ds(i, SUBLANES, stride=pitch)
        half_c = prm_sc[0]
        tr = jnp.tanh(ga_sc[:, rs, :] + prm_sc[1])
        ti = jnp.tanh(gx_sc[:, rs, :] + prm_sc[2])
        log_a = half_c + half_c * tr
        a = jnp.exp(log_a)
        half_a = 0.5 * a
        y = (-0.25 - half_a * half_a) * jnp.tanh(log_a)
        half_beta = jnp.where(y > 0.0, y * lax.rsqrt(y), 0.0)
        x = xc_sc[:, rs, :]
        b = half_beta * (x + x * ti)
        h = a * h + b
        p = a * p
        blk = pl.ds(pl.multiple_of(i * SUBLANES, SUBLANES), SUBLANES)
        hl_sc[:, blk, :] = h
        p_sc[:, blk, :] = p
        return h, p

    h_end, p_end = lax.fori_loop(
        0, seg, local,
        (jnp.zeros(vshape, jnp.float32), jnp.ones(vshape, jnp.float32)), unroll=2)

    sub = lax.broadcasted_iota(jnp.int32, vshape, 1)
    a, b = p_end, h_end
    for d in (1, 2, 4):
        keep = sub >= d
        a_sh = pltpu.roll(a, d, 1)
        b_sh = pltpu.roll(b, d, 1)
        b = jnp.where(keep, a * b_sh + b, b)
        a = jnp.where(keep, a * a_sh, a)
    carry_in = h_sc[...]
    seg_end = a * carry_in + b
    h_in = jnp.where(sub == 0, carry_in, pltpu.roll(seg_end, 1, 1))
    h_sc[...] = jnp.broadcast_to(seg_end[:, SUBLANES - 1:SUBLANES, :], vshape)

    def fix(i, c):
        blk = pl.ds(pl.multiple_of(i * SUBLANES, SUBLANES), SUBLANES)
        xc_sc[:, pl.ds(i, SUBLANES, stride=pitch), :] = hl_sc[:, blk, :] + p_sc[:, blk, :] * h_in
        return c

    lax.fori_loop(0, seg, fix, 0)

    def emit(it, c):
        rs = pl.ds(pl.multiple_of(it * rows, rows), rows)
        ps = padded_rows(it)
        h = jnp.concatenate([xc_sc[j, ps, :] for j in range(N_CT)], axis=-1)
        o_ref[rs, :] = (h * zc_ref[rs, :].astype(jnp.float32)).astype(o_ref.dtype)
        return c

    lax.fori_loop(0, ts // rows, emit, 0)


def _rglru(xbz3, cw, cb, wp, ba, bx, lam, *, ts=256, rows=16):
    b, s, _ = xbz3.shape
    const = pl.Buffered(1)
    vec = lambda: pl.BlockSpec((N_CT, 1, LANES), lambda bi, si: (0, 0, 0))
    tiled = lambda: pltpu.VMEM((N_CT, ts, LANES), jnp.float32)
    padded = lambda: pltpu.VMEM(
        (N_CT, SUBLANES * _segment_pitch(ts // SUBLANES), LANES), jnp.float32)
    tile_major = lambda v: v.reshape(N_CT, 1, LANES)
    return pl.pallas_call(
        functools.partial(_rglru_kernel, ts=ts, rows=rows),
        grid=(b, s // ts),
        in_specs=[
            pl.BlockSpec((None, ts, C_WIDTH), lambda bi, si: (bi, si, 0)),
            pl.BlockSpec((None, ts, C_WIDTH), lambda bi, si: (bi, si, 1)),
            pl.BlockSpec((C_CONV, C_WIDTH), lambda bi, si: (0, 0)),
            pl.BlockSpec((1, C_WIDTH), lambda bi, si: (0, 0)),
            pl.BlockSpec((2, C_HEADS, GATE_PAD, GATE_PAD), lambda bi, si: (0, 0, 0, 0),
                         pipeline_mode=const),
            vec(), vec(), vec(),
        ],
        out_specs=pl.BlockSpec((None, ts, C_WIDTH), lambda bi, si: (bi, si, 0)),
        out_shape=jax.ShapeDtypeStruct((b, s, C_WIDTH), jnp.bfloat16),
        scratch_shapes=[
            pltpu.VMEM((ts + SUBLANES, C_WIDTH), jnp.float32),
            padded(),
            pltpu.VMEM((ts, C_WIDTH), jnp.bfloat16),
            padded(),
            padded(),
            pltpu.VMEM((N_CT, SUBLANES, LANES), jnp.float32),
            pltpu.VMEM((N_CT, GATE_WIN, 2 * LANES), jnp.bfloat16),
            pltpu.VMEM((GATE_WIN, 2 * LANES), jnp.float32),
            tiled(),
            tiled(),
            pltpu.VMEM((3, N_CT, SUBLANES, LANES), jnp.float32),
        ],
        compiler_params=pltpu.CompilerParams(
            dimension_semantics=("arbitrary", "arbitrary"),
            vmem_limit_bytes=VMEM_LIMIT),
        name="rglru",
    )(xbz3, xbz3, cw, cb, wp, tile_major(ba), tile_major(bx), tile_major(lam))


def _out1_kernel(hz_ref, x_ref, w_ref, g_ref, o_ref):
    y = jnp.dot(hz_ref[...], w_ref[...], preferred_element_type=jnp.float32)
    o_ref[...] = _rms_rows(x_ref[...] + y, g_ref[...])


def _out1(hz, x1, w_bf, g, *, tm=256):
    t = x1.shape[0]
    return pl.pallas_call(
        _out1_kernel,
        grid=(t // tm,),
        in_specs=[
            pl.BlockSpec((tm, C_WIDTH), lambda i: (i, 0)),
            pl.BlockSpec((tm, D_MODEL), lambda i: (i, 0)),
            pl.BlockSpec((C_WIDTH, D_MODEL), lambda i: (0, 0),
                         pipeline_mode=pl.Buffered(1)),
            pl.BlockSpec((1, D_MODEL), lambda i: (0, 0)),
        ],
        out_specs=pl.BlockSpec((tm, D_MODEL), lambda i: (i, 0)),
        out_shape=jax.ShapeDtypeStruct((t, D_MODEL), jnp.float32),
        compiler_params=pltpu.CompilerParams(
            dimension_semantics=("arbitrary",),
            vmem_limit_bytes=VMEM_LIMIT),
        name="out1",
    )(hz, x1, w_bf, g)


def kernel(x, ab_norm, ab_w_in, ab_lambda, ab_head_norm, ab_sgu_ln_g, ab_sgu_ln_b, ab_sgu_w, ab_sgu_b, ab_w_out, c_norm, c_w_in, c_conv_w, c_conv_b, c_gate_a_w, c_gate_a_b, c_gate_x_w, c_gate_x_b, c_lambda, c_w_out, final_norm):
    bsz, seq, d = x.shape
    t = bsz * seq
    bf = jnp.bfloat16
    row = lambda v: v.reshape(1, -1)
    x2 = x.reshape(t, d)

    proj, (w_out0, w_in1, w_out1) = _proj0(
        x2, row(ab_norm[0]), ab_w_in[0], [ab_w_out[0], c_w_in[0], c_w_out[0]])
    ya = _attention(proj.reshape(bsz, seq, AB_IN), ab_lambda[0], row(ab_head_norm[0]))
    x1, xn1 = _out0(ya.reshape(t, A_WIDTH), proj, x2, w_out0, row(c_norm[0]),
                    row(ab_sgu_ln_g[0]), row(ab_sgu_ln_b[0]), ab_sgu_w[0], ab_sgu_b[0].T)

    xbz = _proj1(xn1, w_in1)
    padw = GATE_PAD - C_BLOCK
    wp = jnp.pad(jnp.stack([c_gate_a_w[0], c_gate_x_w[0]]).astype(bf),
                 ((0, 0), (0, 0), (0, padw), (0, padw)))
    hz = _rglru(xbz.reshape(bsz, seq, 2 * C_WIDTH), c_conv_w[0], row(c_conv_b[0]), wp,
                row(c_gate_a_b[0]), row(c_gate_x_b[0]), row(c_lambda[0]))
    out = _out1(hz.reshape(t, C_WIDTH), x1, w_out1, row(final_norm))
    return out.reshape(bsz, seq, d)
```

```python
import functools
import math

import jax
import jax.numpy as jnp
import numpy as np
from jax import lax
from jax.experimental import pallas as pl
from jax.experimental.pallas import tpu as pltpu

D_MODEL = 2048
CHUNK = 64
CHUNK_SHIFT = 6
NORM_EPS = 1e-6
A_HEADS = 8
A_HEAD_DIM = 128
A_VDIM = 2 * A_HEAD_DIM
A_WIDTH = A_HEADS * A_VDIM
B_GROUPS = 8
B_WIDTH = D_MODEL
B_GROUP_DIM = B_WIDTH // B_GROUPS
B_SPAN = 128
C_WIDTH = 2688
C_HEADS = 16
C_BLOCK = C_WIDTH // C_HEADS
C_CONV = 4
C_GATE_C = 8.0
AB_IN = 4 * A_WIDTH + 3 * B_WIDTH
LAM_INIT_0 = 0.8 - 0.6 * math.exp(-0.3 * 0)

LANES = 128
SUBLANES = 8
LOG2E = 1.4426950408889634
VMEM_LIMIT = 56 * 1024 * 1024
SIDE_CAST_MAX_STEPS = 64
BF16_ROWS = 16

GATE_WIN = 512
GATE_PAD = 256
N_CT = C_WIDTH // LANES


def _gate_window_start(j):
    h0 = (LANES * j) // C_BLOCK
    ws = (C_BLOCK * h0) // LANES * LANES
    return min(ws, C_WIDTH - GATE_WIN)


GATE_WS = tuple(_gate_window_start(j) for j in range(N_CT))


def _silu(z):
    h = 0.5 * z
    return h + h * jnp.tanh(h)


def _gelu_tanh(x):
    c = math.sqrt(2.0 / math.pi)
    h = 0.5 * x
    return h + h * jnp.tanh(x * (c + (c * 0.044715) * (x * x)))


def _rms_rows(xf, g):
    ms = jnp.mean(xf * xf, axis=-1, keepdims=True)
    return xf * lax.rsqrt(ms + NORM_EPS) * g


def _proj0_kernel(x_ref, g_ref, w_ref, *rest, tiles_per_seg, emit_weights, side_steps):
    j = pl.program_id(1)
    if emit_weights:
        o_ref, wb_ref, xn_sc = rest
        wb_ref[...] = w_ref[...].astype(wb_ref.dtype)
    else:
        n_side = len(side_steps)
        side_in, o_ref = rest[:n_side], rest[n_side + 1]
        side_out, xn_sc = rest[n_side + 2:2 * n_side + 2], rest[-1]
        step = pl.program_id(0) * pl.num_programs(1) + j
        for src, dst, steps in zip(side_in, side_out, side_steps):
            @pl.when(step < steps)
            def _(src=src, dst=dst):
                dst[...] = src[...].astype(dst.dtype)

    @pl.when(j == 0)
    def _():
        xn_sc[...] = _rms_rows(x_ref[...], g_ref[...]).astype(xn_sc.dtype)

    seg = j // tiles_per_seg

    def proj():
        return jnp.dot(xn_sc[...], w_ref[...], preferred_element_type=jnp.float32)

    @pl.when(seg <= 2)
    def _():
        scale = jnp.where(seg == 0, A_HEAD_DIM ** -0.5 * LOG2E, 1.0).astype(jnp.float32)
        o_ref[...] = (proj() * scale).astype(o_ref.dtype)

    @pl.when((seg == 3) | (seg == 6))
    def _():
        o_ref[...] = _silu(proj()).astype(o_ref.dtype)

    @pl.when((seg == 4) | (seg == 5))
    def _():
        o_ref[...] = _gelu_tanh(proj()).astype(o_ref.dtype)


def _side_cast_steps(rows):
    return max(s for s in range(1, SIDE_CAST_MAX_STEPS + 1) if rows % (s * BF16_ROWS) == 0)


def _proj0(x2, g, w, later_weights, *, tm=1024, tn=1024):
    t = x2.shape[0]
    n = w.shape[1]
    grid_body = (t // tm - 1, n // tn)
    side = list(later_weights)
    side_steps = tuple(_side_cast_steps(a.shape[0]) for a in side)
    assert grid_body[0] * grid_body[1] >= max(side_steps)
    side_specs = [
        pl.BlockSpec((a.shape[0] // steps, a.shape[1]),
                     lambda i, j, steps=steps: (jnp.minimum(i * grid_body[1] + j, steps - 1), 0))
        for a, steps in zip(side, side_steps)]
    params = pltpu.CompilerParams(
        dimension_semantics=("arbitrary", "arbitrary"), vmem_limit_bytes=VMEM_LIMIT)
    x_spec = lambda first: pl.BlockSpec((tm, D_MODEL), lambda i, j: (i + first, 0))
    g_spec = pl.BlockSpec((1, D_MODEL), lambda i, j: (0, 0))
    w_spec = pl.BlockSpec((D_MODEL, tn), lambda i, j: (0, j))
    o_spec = lambda first: pl.BlockSpec((tm, tn), lambda i, j: (i + first, j))
    proj_shape = jax.ShapeDtypeStruct((t, n), jnp.bfloat16)
    proj, w_bf = pl.pallas_call(
        functools.partial(_proj0_kernel, tiles_per_seg=A_WIDTH // tn, emit_weights=True,
                          side_steps=()),
        grid=(1, n // tn),
        in_specs=[x_spec(0), g_spec, w_spec],
        out_specs=[o_spec(0), w_spec],
        out_shape=[proj_shape, jax.ShapeDtypeStruct(w.shape, jnp.bfloat16)],
        scratch_shapes=[pltpu.VMEM((tm, D_MODEL), w.dtype)],
        compiler_params=params,
        name="proj0_head",
    )(x2, g, w)
    outs = pl.pallas_call(
        functools.partial(_proj0_kernel, tiles_per_seg=A_WIDTH // tn, emit_weights=False,
                          side_steps=side_steps),
        grid=grid_body,
        in_specs=[x_spec(1), g_spec, w_spec] + side_specs
                 + [pl.BlockSpec(memory_space=pl.ANY)],
        out_specs=[o_spec(1)] + side_specs,
        out_shape=[proj_shape] + [jax.ShapeDtypeStruct(a.shape, jnp.bfloat16) for a in side],
        input_output_aliases={3 + len(side): 0},
        scratch_shapes=[pltpu.VMEM((tm, D_MODEL), jnp.bfloat16)],
        compiler_params=params,
        name="proj0",
    )(x2, g, w_bf, *side, proj)
    return outs[0], outs[1:]


ATT_WIDE = 1


def _attn_schedule(nq):
    wide = [(i, j, ATT_WIDE, False)
            for j in range(0, nq, ATT_WIDE) for i in range(j + ATT_WIDE, nq)]
    narrow = [(i, j, 1, False)
              for i in range(nq) for j in range(i - i % ATT_WIDE, i)]
    diag = [(i, i, 1, True) for i in range(nq)]
    steps = wide + narrow + diag
    assert len(steps) % 2 == 0
    first_visit = {}
    for n, st in enumerate(steps):
        first_visit.setdefault(st[0], n)
    flags = []
    for n, (i, _, w, masked) in enumerate(steps):
        nxt = steps[n + 1][2] if n + 1 < len(steps) else 0
        flags.append((w, masked, first_visit[i] == n, nxt))
    pairs = [(flags[n], flags[n + 1]) for n in range(0, len(steps), 2)]
    runs = []
    for k, pf in enumerate(pairs):
        if runs and runs[-1][2] == pf:
            runs[-1] = (runs[-1][0], runs[-1][1] + 1, pf)
        else:
            runs.append((k, 1, pf))
    return steps, tuple(runs)


def _attn_kernel(it_ref, jt_ref, lamp_ref, q_ref, k_ref, v_ref, za_ref, hn_ref, o_ref,
                 m_sc, l_sc, acc_sc, s0_sc, s1_sc, *, t, pair_runs):
    s_bufs = (s0_sc, s1_sc)
    nt = (((1,), (1,)), ((), ()))

    lp = lamp_ref[...]
    d1 = jnp.sum(lp[0:1, :] * lp[1:2, :], axis=1, keepdims=True)
    d2 = jnp.sum(lp[2:3, :] * lp[3:4, :], axis=1, keepdims=True)
    lam = jnp.exp(d1) - jnp.exp(d2) + LAM_INIT_0

    def rows_of(tile, tiles=1):
        return pl.ds(pl.multiple_of(tile * t, t), tiles * t)

    def scores(n, width, s_ref):
        qs, ks = rows_of(it_ref[n]), rows_of(jt_ref[n], width)
        s_ref[0:t, 0:width * t] = lax.dot_general(
            q_ref[qs, :A_HEAD_DIM], k_ref[ks, :A_HEAD_DIM], nt,
            preferred_element_type=jnp.float32)
        s_ref[t:2 * t, 0:width * t] = lax.dot_general(
            q_ref[qs, A_HEAD_DIM:], k_ref[ks, A_HEAD_DIM:], nt,
            preferred_element_type=jnp.float32)

    def softmax_pv(n, s_ref, width, masked, first):
        i = it_ref[n]
        keys = width * t
        s = s_ref[:, 0:keys]
        if masked:
            rq = lax.broadcasted_iota(jnp.int32, (2 * t, 1), 0)
            rq = lax.shift_right_logical(jnp.where(rq >= t, rq - t, rq), CHUNK_SHIFT)
            ck = lax.shift_right_logical(
                lax.broadcasted_iota(jnp.int32, (1, keys), 1), CHUNK_SHIFT)
            s = jnp.where(ck <= rq, s, -jnp.inf)
        v = v_ref[rows_of(jt_ref[n], width), :]
        if first:
            m_new = jnp.broadcast_to(jnp.max(s, axis=1, keepdims=True), (2 * t, LANES))
            p = jnp.exp2(s - pltpu.repeat(m_new, keys // LANES, 1))
            l_sc[i] = jnp.broadcast_to(jnp.sum(p, axis=1, keepdims=True), (2 * t, LANES))
            m_sc[i] = m_new
            acc_sc[i] = jnp.dot(p.astype(jnp.bfloat16), v, preferred_element_type=jnp.float32)
            return
        m_prev = m_sc[i]
        m_new = jnp.maximum(m_prev, jnp.max(s, axis=1, keepdims=True))
        alpha = jnp.exp2(m_prev - m_new)
        p = jnp.exp2(s - pltpu.repeat(m_new, keys // LANES, 1))
        l_sc[i] = alpha * l_sc[i] + jnp.sum(p, axis=1, keepdims=True)
        m_sc[i] = m_new
        pv = jnp.dot(p.astype(jnp.bfloat16), v,
                     preferred_element_type=jnp.float32)
        acc_sc[i] = acc_sc[i] * pltpu.repeat(alpha, A_VDIM // LANES, 1) + pv

    def finish(i):
        inv_l = 1.0 / l_sc[i]
        acc = acc_sc[i] * pltpu.repeat(inv_l, A_VDIM // LANES, 1)
        o = acc[:t, :] - lam * acc[t:, :]
        ms = jnp.mean(o * o, axis=-1, keepdims=True)
        o = o * lax.rsqrt(ms + NORM_EPS) * hn_ref[...] * (1.0 - LAM_INIT_0)
        rows = rows_of(i)
        o_ref[rows, :] = (o * za_ref[rows, :].astype(jnp.float32)).astype(o_ref.dtype)

    def tick(n, par, width, masked, first, next_width):
        if next_width:
            scores(n + 1, next_width, s_bufs[1 - par])
        softmax_pv(n, s_bufs[par], width, masked, first)
        if masked:
            finish(it_ref[n])

    scores(0, pair_runs[0][2][0][0], s0_sc)
    for start, count, flags in pair_runs:
        def pair(pp, c, start=start, flags=flags):
            n = 2 * (start + pp)
            tick(n, 0, *flags[0])
            tick(n + 1, 1, *flags[1])
            return c
        if count == 1:
            pair(0, 0)
        else:
            lax.fori_loop(0, count, pair, 0)


def _attention(proj3, lamp, hn, *, t=512):
    b, s, _ = proj3.shape
    nblk = A_WIDTH // A_VDIM
    nq = s // t
    steps, pair_runs = _attn_schedule(nq)
    it = jnp.asarray([st[0] for st in steps], jnp.int32)
    jt = jnp.asarray([st[1] for st in steps], jnp.int32)
    head = lambda blk: pl.BlockSpec((None, s, A_VDIM),
                                    lambda bi, h, it_r, jt_r: (bi, 0, blk * nblk + h))
    grid_spec = pltpu.PrefetchScalarGridSpec(
        num_scalar_prefetch=2,
        grid=(b, A_HEADS),
        in_specs=[
            pl.BlockSpec((4, A_HEAD_DIM), lambda bi, h, it_r, jt_r: (0, 0)),
            head(0), head(1), head(2), head(3),
            pl.BlockSpec((1, A_VDIM), lambda bi, h, it_r, jt_r: (0, h)),
        ],
        out_specs=pl.BlockSpec((None, s, A_VDIM), lambda bi, h, it_r, jt_r: (bi, 0, h)),
        scratch_shapes=[
            pltpu.VMEM((nq, 2 * t, LANES), jnp.float32),
            pltpu.VMEM((nq, 2 * t, LANES), jnp.float32),
            pltpu.VMEM((nq, 2 * t, A_VDIM), jnp.float32),
            pltpu.VMEM((2 * t, ATT_WIDE * t), jnp.float32),
            pltpu.VMEM((2 * t, ATT_WIDE * t), jnp.float32),
        ],
    )
    return pl.pallas_call(
        functools.partial(_attn_kernel, t=t, pair_runs=pair_runs),
        grid_spec=grid_spec,
        out_shape=jax.ShapeDtypeStruct((b, s, A_WIDTH), jnp.bfloat16),
        compiler_params=pltpu.CompilerParams(
            dimension_semantics=("arbitrary", "arbitrary"),
            vmem_limit_bytes=VMEM_LIMIT),
        name="diff_attn",
    )(it, jt, lamp, proj3, proj3, proj3, proj3, hn)


def _out0_kernel(ya_ref, gu_ref, gv_ref, zb_ref, x_ref, wa_ref, wb_ref, g_ref,
                 lng_ref, lnb_ref, ws_ref, bst_ref, x1_ref, xn_ref, yb_sc, *, rows):
    cols = [slice(n * B_GROUP_DIM, (n + 1) * B_GROUP_DIM) for n in range(B_GROUPS)]
    lead = 2

    def attn_half(cs):
        x1_ref[:, cs] = x_ref[:, cs] + jnp.dot(ya_ref[...], wa_ref[:, cs],
                                              preferred_element_type=jnp.float32)

    for n in range(lead):
        attn_half(cols[n])
    v = gv_ref[...].astype(jnp.float32)
    mu = jnp.mean(v, axis=-1, keepdims=True)
    vc = v - mu
    var = jnp.mean(vc * vc, axis=-1, keepdims=True)
    rstd = lax.rsqrt(var + NORM_EPS)
    ci = lax.broadcasted_iota(jnp.int32, (B_SPAN, B_SPAN), 0) // CHUNK
    cj = lax.broadcasted_iota(jnp.int32, (B_SPAN, B_SPAN), 1) // CHUNK
    keep = ci >= cj
    for g, cs in enumerate(cols):
        vln = (vc[:, cs] * rstd * lng_ref[:, cs] + lnb_ref[:, cs]).astype(jnp.bfloat16)
        w = jnp.where(keep, ws_ref[g], 0.0).astype(jnp.bfloat16)
        bcol = bst_ref[:, g:g + 1]
        for sp in range(rows // B_SPAN):
            rs = slice(sp * B_SPAN, (sp + 1) * B_SPAN)
            mixed = jnp.dot(w, vln[rs, :], preferred_element_type=jnp.float32) + bcol
            u = gu_ref[rs, cs].astype(jnp.float32)
            z = zb_ref[rs, cs].astype(jnp.float32)
            yb_sc[rs, cs] = (u * mixed * z).astype(yb_sc.dtype)
        if g + lead < B_GROUPS:
            attn_half(cols[g + lead])
    ssq = jnp.zeros((rows, 1), jnp.float32)
    for cs in cols:
        x1 = x1_ref[:, cs] + jnp.dot(yb_sc[...], wb_ref[:, cs],
                                     preferred_element_type=jnp.float32)
        x1_ref[:, cs] = x1
        ssq = ssq + jnp.sum(x1 * x1, axis=-1, keepdims=True)
    scale = lax.rsqrt(ssq * (1.0 / D_MODEL) + NORM_EPS)
    xn_ref[...] = (x1_ref[...] * scale * g_ref[...]).astype(xn_ref.dtype)


def _out0(ya, proj, x2, w_bf, g, ln_g, ln_b, w_s, b_st, *, tm=256):
    t = x2.shape[0]
    const = pl.Buffered(1)
    base = 4 * A_WIDTH // B_WIDTH
    vec = lambda n: pl.BlockSpec((1, n), lambda i: (0, 0))
    return pl.pallas_call(
        functools.partial(_out0_kernel, rows=tm),
        grid=(t // tm,),
        in_specs=[
            pl.BlockSpec((tm, A_WIDTH), lambda i: (i, 0)),
            pl.BlockSpec((tm, B_WIDTH), lambda i: (i, base)),
            pl.BlockSpec((tm, B_WIDTH), lambda i: (i, base + 1)),
            pl.BlockSpec((tm, B_WIDTH), lambda i: (i, base + 2)),
            pl.BlockSpec((tm, D_MODEL), lambda i: (i, 0)),
            pl.BlockSpec((A_WIDTH, D_MODEL), lambda i: (0, 0), pipeline_mode=const),
            pl.BlockSpec((B_WIDTH, D_MODEL), lambda i: (1, 0), pipeline_mode=const),
            vec(D_MODEL), vec(B_WIDTH), vec(B_WIDTH),
            pl.BlockSpec((B_GROUPS, B_SPAN, B_SPAN), lambda i: (0, 0, 0)),
            pl.BlockSpec((B_SPAN, B_GROUPS), lambda i: (0, 0)),
        ],
        out_specs=[
            pl.BlockSpec((tm, D_MODEL), lambda i: (i, 0)),
            pl.BlockSpec((tm, D_MODEL), lambda i: (i, 0)),
        ],
        out_shape=[
            jax.ShapeDtypeStruct((t, D_MODEL), jnp.float32),
            jax.ShapeDtypeStruct((t, D_MODEL), jnp.bfloat16),
        ],
        scratch_shapes=[pltpu.VMEM((tm, B_WIDTH), jnp.bfloat16)],
        compiler_params=pltpu.CompilerParams(
            dimension_semantics=("arbitrary",),
            vmem_limit_bytes=VMEM_LIMIT),
        name="out0",
    )(ya, proj, proj, proj, x2, w_bf, w_bf, g, ln_g, ln_b, w_s, b_st)


def _proj1_kernel(xn_ref, w_ref, o_ref):
    def proj():
        return jnp.dot(xn_ref[...], w_ref[...], preferred_element_type=jnp.float32)

    @pl.when(pl.program_id(0) == 0)
    def _():
        o_ref[...] = proj().astype(o_ref.dtype)

    @pl.when(pl.program_id(0) == 1)
    def _():
        o_ref[...] = _silu(proj()).astype(o_ref.dtype)


def _proj1(xn, w_bf, *, tm=512):
    t = xn.shape[0]
    return pl.pallas_call(
        _proj1_kernel,
        grid=(2, t // tm),
        in_specs=[
            pl.BlockSpec((tm, D_MODEL), lambda j, i: (i, 0)),
            pl.BlockSpec((D_MODEL, C_WIDTH), lambda j, i: (0, j)),
        ],
        out_specs=pl.BlockSpec((tm, C_WIDTH), lambda j, i: (i, j)),
        out_shape=jax.ShapeDtypeStruct((t, 2 * C_WIDTH), jnp.bfloat16),
        compiler_params=pltpu.CompilerParams(
            dimension_semantics=("arbitrary", "arbitrary"),
            vmem_limit_bytes=VMEM_LIMIT),
        name="proj1",
    )(xn, w_bf)


def _segment_pitch(seg):
    units = seg // SUBLANES
    return SUBLANES * (units + 1 if units % 2 == 0 else units)


def _build_gate_slabs(wp_ref, wg_sc, tmp_sc):
    row = lax.broadcasted_iota(jnp.int32, (GATE_PAD, LANES), 0)
    col = lax.broadcasted_iota(jnp.int32, (GATE_PAD, LANES), 1)
    for j in range(N_CT):
        tmp_sc[...] = jnp.zeros(tmp_sc.shape, jnp.float32)
        h_lo = (LANES * j) // C_BLOCK
        h_hi = (LANES * j + LANES - 1) // C_BLOCK
        for gate in range(2):
            for h in range(h_lo, h_hi + 1):
                sel = jnp.where(row == col + (LANES * j - C_BLOCK * h), 1.0, 0.0)
                part = jnp.dot(wp_ref[gate, h], sel.astype(jnp.bfloat16),
                               preferred_element_type=jnp.float32)
                r0 = C_BLOCK * h - GATE_WS[j]
                cs = slice(gate * LANES, (gate + 1) * LANES)
                tmp_sc[r0:r0 + C_BLOCK, cs] = tmp_sc[r0:r0 + C_BLOCK, cs] + part[:C_BLOCK, :]
        wg_sc[j] = (0.5 * tmp_sc[...]).astype(jnp.bfloat16)


def _rglru_kernel(xb_ref, zc_ref, cw_ref, cb_ref, wp_ref, ba_ref, bx_ref,
                  lam_ref, o_ref, pad_sc, xc_sc, xcb_sc, ga_sc, gx_sc, h_sc,
                  wg_ref, tmp_sc, hl_sc, p_sc, prm_sc, *, ts, rows):
    si = pl.program_id(1)
    halo = SUBLANES
    vshape = (N_CT, SUBLANES, LANES)

    @pl.when((pl.program_id(0) == 0) & (si == 0))
    def _():
        _build_gate_slabs(wp_ref, wg_ref, tmp_sc)
        lam = lam_ref[...]
        sp = jnp.maximum(-lam, 0.0) + jnp.log1p(jnp.exp(-jnp.abs(lam)))
        prm_sc[0] = jnp.broadcast_to((-0.5 * C_GATE_C) * sp, vshape)
        prm_sc[1] = jnp.broadcast_to(0.5 * ba_ref[...], vshape)
        prm_sc[2] = jnp.broadcast_to(0.5 * bx_ref[...], vshape)

    @pl.when(si == 0)
    def _():
        pad_sc[0:halo, :] = jnp.zeros((halo, C_WIDTH), jnp.float32)
        h_sc[...] = jnp.zeros(h_sc.shape, jnp.float32)

    @pl.when(si > 0)
    def _():
        pad_sc[0:halo, :] = pad_sc[ts:ts + halo, :]

    seg = ts // SUBLANES
    pitch = _segment_pitch(seg)

    def put(dst_sc, j, val):
        for s in range(SUBLANES):
            dst_sc[j, s * pitch:s * pitch + seg, :] = val[s * seg:(s + 1) * seg, :]

    win_tiles = GATE_WIN // LANES
    for c in range(N_CT):
        cs = slice(c * LANES, (c + 1) * LANES)
        pad_sc[halo:halo + ts, cs] = xb_ref[:, cs].astype(jnp.float32)
        xc = cb_ref[:, cs] + cw_ref[C_CONV - 1:C_CONV, cs] * pad_sc[halo:halo + ts, cs]
        for tap in range(C_CONV - 1):
            off = halo - (C_CONV - 1) + tap
            xc = xc + cw_ref[tap:tap + 1, cs] * pad_sc[off:off + ts, cs]
        xcb_sc[:, cs] = xc.astype(jnp.bfloat16)
        put(xc_sc, c, xc)
        for j in range(N_CT):
            ws = GATE_WS[j]
            if ws // LANES + win_tiles - 1 == c:
                g = jnp.dot(xcb_sc[:, ws:ws + GATE_WIN], wg_ref[j],
                            preferred_element_type=jnp.float32)
                put(ga_sc, j, g[:, :LANES])
                put(gx_sc, j, g[:, LANES:])

    blocks_per_seg = seg // rows

    def padded_rows(it):
        start = (it // blocks_per_seg) * pitch + (it % blocks_per_seg) * rows
        return pl.ds(pl.multiple_of(start, SUBLANES), rows)

    def local(i, carry):
        h, p = carry
        rs = pl.ds(i, SUBLANES, stride=pitch)
        half_c = prm_sc[0]
        tr = jnp.tanh(ga_sc[:, rs, :] + prm_sc[1])
        ti = jnp.tanh(gx_sc[:, rs, :] + prm_sc[2])
        log_a = half_c + half_c * tr
        a = jnp.exp(log_a)
        half_a = 0.5 * a
        y = (-0.25 - half_a * half_a) * jnp.tanh(log_a)
        half_beta = jnp.where(y > 0.0, y * lax.rsqrt(y), 0.0)
        x = xc_sc[:, rs, :]
        b = half_beta * (x + x * ti)
        h = a * h + b
        p = a * p
        blk = pl.ds(pl.multiple_of(i * SUBLANES, SUBLANES), SUBLANES)
        hl_sc[:, blk, :] = h
        p_sc[:, blk, :] = p
        return h, p

    h_end, p_end = lax.fori_loop(
        0, seg, local,
        (jnp.zeros(vshape, jnp.float32), jnp.ones(vshape, jnp.float32)), unroll=2)

    sub = lax.broadcasted_iota(jnp.int32, vshape, 1)
    a, b = p_end, h_end
    for d in (1, 2, 4):
        keep = sub >= d
        a_sh = pltpu.roll(a, d, 1)
        b_sh = pltpu.roll(b, d, 1)
        b = jnp.where(keep, a * b_sh + b, b)
        a = jnp.where(keep, a * a_sh, a)
    carry_in = h_sc[...]
    seg_end = a * carry_in + b
    h_in = jnp.where(sub == 0, carry_in, pltpu.roll(seg_end, 1, 1))
    h_sc[...] = jnp.broadcast_to(seg_end[:, SUBLANES - 1:SUBLANES, :], vshape)

    def fix(i, c):
        blk = pl.ds(pl.multiple_of(i * SUBLANES, SUBLANES), SUBLANES)
        xc_sc[:, pl.ds(i, SUBLANES, stride=pitch), :] = hl_sc[:, blk, :] + p_sc[:, blk, :] * h_in
        return c

    lax.fori_loop(0, seg, fix, 0)

    def emit(it, c):
        rs = pl.ds(pl.multiple_of(it * rows, rows), rows)
        ps = padded_rows(it)
        h = jnp.concatenate([xc_sc[j, ps, :] for j in range(N_CT)], axis=-1)
        o_ref[rs, :] = (h * zc_ref[rs, :].astype(jnp.float32)).astype(o_ref.dtype)
        return c

    lax.fori_loop(0, ts // rows, emit, 0)


def _rglru(xbz3, cw, cb, wp, ba, bx, lam, *, ts=256, rows=16):
    b, s, _ = xbz3.shape
    const = pl.Buffered(1)
    vec = lambda: pl.BlockSpec((N_CT, 1, LANES), lambda bi, si: (0, 0, 0))
    tiled = lambda: pltpu.VMEM((N_CT, ts, LANES), jnp.float32)
    padded = lambda: pltpu.VMEM(
        (N_CT, SUBLANES * _segment_pitch(ts // SUBLANES), LANES), jnp.float32)
    tile_major = lambda v: v.reshape(N_CT, 1, LANES)
    return pl.pallas_call(
        functools.partial(_rglru_kernel, ts=ts, rows=rows),
        grid=(b, s // ts),
        in_specs=[
            pl.BlockSpec((None, ts, C_WIDTH), lambda bi, si: (bi, si, 0)),
            pl.BlockSpec((None, ts, C_WIDTH), lambda bi, si: (bi, si, 1)),
            pl.BlockSpec((C_CONV, C_WIDTH), lambda bi, si: (0, 0)),
            pl.BlockSpec((1, C_WIDTH), lambda bi, si: (0, 0)),
            pl.BlockSpec((2, C_HEADS, GATE_PAD, GATE_PAD), lambda bi, si: (0, 0, 0, 0),
                         pipeline_mode=const),
            vec(), vec(), vec(),
        ],
        out_specs=pl.BlockSpec((None, ts, C_WIDTH), lambda bi, si: (bi, si, 0)),
        out_shape=jax.ShapeDtypeStruct((b, s, C_WIDTH), jnp.bfloat16),
        scratch_shapes=[
            pltpu.VMEM((ts + SUBLANES, C_WIDTH), jnp.float32),
            padded(),
            pltpu.VMEM((ts, C_WIDTH), jnp.bfloat16),
            padded(),
            padded(),
            pltpu.VMEM((N_CT, SUBLANES, LANES), jnp.float32),
            pltpu.VMEM((N_CT, GATE_WIN, 2 * LANES), jnp.bfloat16),
            pltpu.VMEM((GATE_WIN, 2 * LANES), jnp.float32),
            tiled(),
            tiled(),
            pltpu.VMEM((3, N_CT, SUBLANES, LANES), jnp.float32),
        ],
        compiler_params=pltpu.CompilerParams(
            dimension_semantics=("arbitrary", "arbitrary"),
            vmem_limit_bytes=VMEM_LIMIT),
        name="rglru",
    )(xbz3, xbz3, cw, cb, wp, tile_major(ba), tile_major(bx), tile_major(lam))


def _out1_kernel(hz_ref, x_ref, w_ref, g_ref, o_ref):
    y = jnp.dot(hz_ref[...], w_ref[...], preferred_element_type=jnp.float32)
    o_ref[...] = _rms_rows(x_ref[...] + y, g_ref[...])


def _out1(hz, x1, w_bf, g, *, tm=256):
    t = x1.shape[0]
    return pl.pallas_call(
        _out1_kernel,
        grid=(t // tm,),
        in_specs=[
            pl.BlockSpec((tm, C_WIDTH), lambda i: (i, 0)),
            pl.BlockSpec((tm, D_MODEL), lambda i: (i, 0)),
            pl.BlockSpec((C_WIDTH, D_MODEL), lambda i: (0, 0),
                         pipeline_mode=pl.Buffered(1)),
            pl.BlockSpec((1, D_MODEL), lambda i: (0, 0)),
        ],
        out_specs=pl.BlockSpec((tm, D_MODEL), lambda i: (i, 0)),
        out_shape=jax.ShapeDtypeStruct((t, D_MODEL), jnp.float32),
        compiler_params=pltpu.CompilerParams(
            dimension_semantics=("arbitrary",),
            vmem_limit_bytes=VMEM_LIMIT),
        name="out1",
    )(hz, x1, w_bf, g)


def kernel(x, ab_norm, ab_w_in, ab_lambda, ab_head_norm, ab_sgu_ln_g, ab_sgu_ln_b, ab_sgu_w, ab_sgu_b, ab_w_out, c_norm, c_w_in, c_conv_w, c_conv_b, c_gate_a_w, c_gate_a_b, c_gate_x_w, c_gate_x_b, c_lambda, c_w_out, final_norm):
    bsz, seq, d = x.shape
    t = bsz * seq
    bf = jnp.bfloat16
    row = lambda v: v.reshape(1, -1)
    x2 = x.reshape(t, d)

    proj, (w_out0, w_in1, w_out1) = _proj0(
        x2, row(ab_norm[0]), ab_w_in[0], [ab_w_out[0], c_w_in[0], c_w_out[0]])
    ya = _attention(proj.reshape(bsz, seq, AB_IN), ab_lambda[0], row(ab_head_norm[0]))
    x1, xn1 = _out0(ya.reshape(t, A_WIDTH), proj, x2, w_out0, row(c_norm[0]),
                    row(ab_sgu_ln_g[0]), row(ab_sgu_ln_b[0]), ab_sgu_w[0], ab_sgu_b[0].T)

    xbz = _proj1(xn1, w_in1)
    padw = GATE_PAD - C_BLOCK
    wp = jnp.pad(jnp.stack([c_gate_a_w[0], c_gate_x_w[0]]).astype(bf),
                 ((0, 0), (0, 0), (0, padw), (0, padw)))
    hz = _rglru(xbz.reshape(bsz, seq, 2 * C_WIDTH), c_conv_w[0], row(c_conv_b[0]), wp,
                row(c_gate_a_b[0]), row(c_gate_x_b[0]), row(c_lambda[0]))
    out = _out1(hz.reshape(t, C_WIDTH), x1, w_out1, row(final_norm))
    return out.reshape(bsz, seq, d)
```

```python
import functools
import math

import jax
import jax.numpy as jnp
import numpy as np
from jax import lax
from jax.experimental import pallas as pl
from jax.experimental.pallas import tpu as pltpu

D_MODEL = 2048
CHUNK = 64
CHUNK_SHIFT = 6
NORM_EPS = 1e-6
A_HEADS = 8
A_HEAD_DIM = 128
A_VDIM = 2 * A_HEAD_DIM
A_WIDTH = A_HEADS * A_VDIM
B_GROUPS = 8
B_WIDTH = D_MODEL
B_GROUP_DIM = B_WIDTH // B_GROUPS
B_SPAN = 128
C_WIDTH = 2688
C_HEADS = 16
C_BLOCK = C_WIDTH // C_HEADS
C_CONV = 4
C_GATE_C = 8.0
AB_IN = 4 * A_WIDTH + 3 * B_WIDTH
LAM_INIT_0 = 0.8 - 0.6 * math.exp(-0.3 * 0)

LANES = 128
SUBLANES = 8
LOG2E = 1.4426950408889634
VMEM_LIMIT = 56 * 1024 * 1024
SIDE_CAST_MAX_STEPS = 64
BF16_ROWS = 16

GATE_WIN = 512
GATE_PAD = 256
N_CT = C_WIDTH // LANES


def _gate_window_start(j):
    h0 = (LANES * j) // C_BLOCK
    ws = (C_BLOCK * h0) // LANES * LANES
    return min(ws, C_WIDTH - GATE_WIN)


GATE_WS = tuple(_gate_window_start(j) for j in range(N_CT))


def _silu(z):
    h = 0.5 * z
    return h + h * jnp.tanh(h)


def _gelu_tanh(x):
    c = math.sqrt(2.0 / math.pi)
    h = 0.5 * x
    return h + h * jnp.tanh(x * (c + (c * 0.044715) * (x * x)))


def _rms_rows(xf, g):
    ms = jnp.mean(xf * xf, axis=-1, keepdims=True)
    return xf * lax.rsqrt(ms + NORM_EPS) * g


def _proj0_kernel(x_ref, g_ref, w_ref, *rest, tiles_per_seg, emit_weights, side_steps):
    j = pl.program_id(1)
    if emit_weights:
        o_ref, wb_ref, xn_sc = rest
        wb_ref[...] = w_ref[...].astype(wb_ref.dtype)
    else:
        n_side = len(side_steps)
        side_in, o_ref = rest[:n_side], rest[n_side + 1]
        side_out, xn_sc = rest[n_side + 2:2 * n_side + 2], rest[-1]
        step = pl.program_id(0) * pl.num_programs(1) + j
        for src, dst, steps in zip(side_in, side_out, side_steps):
            @pl.when(step < steps)
            def _(src=src, dst=dst):
                dst[...] = src[...].astype(dst.dtype)

    @pl.when(j == 0)
    def _():
        xn_sc[...] = _rms_rows(x_ref[...], g_ref[...]).astype(xn_sc.dtype)

    seg = j // tiles_per_seg

    def proj():
        return jnp.dot(xn_sc[...], w_ref[...], preferred_element_type=jnp.float32)

    @pl.when(seg <= 2)
    def _():
        scale = jnp.where(seg == 0, A_HEAD_DIM ** -0.5 * LOG2E, 1.0).astype(jnp.float32)
        o_ref[...] = (proj() * scale).astype(o_ref.dtype)

    @pl.when((seg == 3) | (seg == 6))
    def _():
        o_ref[...] = _silu(proj()).astype(o_ref.dtype)

    @pl.when((seg == 4) | (seg == 5))
    def _():
        o_ref[...] = _gelu_tanh(proj()).astype(o_ref.dtype)


def _side_cast_steps(rows):
    return max(s for s in range(1, SIDE_CAST_MAX_STEPS + 1) if rows % (s * BF16_ROWS) == 0)


def _proj0(x2, g, w, later_weights, *, tm=1024, tn=1024, head_tm=2048, head_tn=512):
    t = x2.shape[0]
    n = w.shape[1]
    head_tiles = head_tm // tm
    grid_body = (t // tm - head_tiles, n // tn)
    side = list(later_weights)
    side_steps = tuple(_side_cast_steps(a.shape[0]) for a in side)
    assert grid_body[0] * grid_body[1] >= max(side_steps)
    side_specs = [
        pl.BlockSpec((a.shape[0] // steps, a.shape[1]),
                     lambda i, j, steps=steps: (jnp.minimum(i * grid_body[1] + j, steps - 1), 0))
        for a, steps in zip(side, side_steps)]
    params = pltpu.CompilerParams(
        dimension_semantics=("arbitrary", "arbitrary"), vmem_limit_bytes=VMEM_LIMIT)
    g_spec = pl.BlockSpec((1, D_MODEL), lambda i, j: (0, 0))
    proj_shape = jax.ShapeDtypeStruct((t, n), jnp.bfloat16)
    head_w_spec = pl.BlockSpec((D_MODEL, head_tn), lambda i, j: (0, j))
    proj, w_bf = pl.pallas_call(
        functools.partial(_proj0_kernel, tiles_per_seg=A_WIDTH // head_tn, emit_weights=True,
                          side_steps=()),
        grid=(1, n // head_tn),
        in_specs=[pl.BlockSpec((head_tm, D_MODEL), lambda i, j: (0, 0),
                               pipeline_mode=pl.Buffered(1)),
                  g_spec, head_w_spec],
        out_specs=[pl.BlockSpec((head_tm, head_tn), lambda i, j: (0, j)), head_w_spec],
        out_shape=[proj_shape, jax.ShapeDtypeStruct(w.shape, jnp.bfloat16)],
        scratch_shapes=[pltpu.VMEM((head_tm, D_MODEL), w.dtype)],
        compiler_params=params,
        name="proj0_head",
    )(x2, g, w)
    x_spec = pl.BlockSpec((tm, D_MODEL), lambda i, j: (i + head_tiles, 0))
    w_spec = pl.BlockSpec((D_MODEL, tn), lambda i, j: (0, j))
    o_spec = pl.BlockSpec((tm, tn), lambda i, j: (i + head_tiles, j))
    outs = pl.pallas_call(
        functools.partial(_proj0_kernel, tiles_per_seg=A_WIDTH // tn, emit_weights=False,
                          side_steps=side_steps),
        grid=grid_body,
        in_specs=[x_spec, g_spec, w_spec] + side_specs
                 + [pl.BlockSpec(memory_space=pl.ANY)],
        out_specs=[o_spec] + side_specs,
        out_shape=[proj_shape] + [jax.ShapeDtypeStruct(a.shape, jnp.bfloat16) for a in side],
        input_output_aliases={3 + len(side): 0},
        scratch_shapes=[pltpu.VMEM((tm, D_MODEL), jnp.bfloat16)],
        compiler_params=params,
        name="proj0",
    )(x2, g, w_bf, *side, proj)
    return outs[0], outs[1:]


ATT_WIDE = 1


def _attn_schedule(nq):
    wide = [(i, j, ATT_WIDE, False)
            for j in range(0, nq, ATT_WIDE) for i in range(j + ATT_WIDE, nq)]
    narrow = [(i, j, 1, False)
              for i in range(nq) for j in range(i - i % ATT_WIDE, i)]
    diag = [(i, i, 1, True) for i in range(nq)]
    steps = wide + narrow + diag
    assert len(steps) % 2 == 0
    first_visit = {}
    for n, st in enumerate(steps):
        first_visit.setdefault(st[0], n)
    flags = []
    for n, (i, _, w, masked) in enumerate(steps):
        nxt = steps[n + 1][2] if n + 1 < len(steps) else 0
        flags.append((w, masked, first_visit[i] == n, nxt))
    pairs = [(flags[n], flags[n + 1]) for n in range(0, len(steps), 2)]
    runs = []
    for k, pf in enumerate(pairs):
        if runs and runs[-1][2] == pf:
            runs[-1] = (runs[-1][0], runs[-1][1] + 1, pf)
        else:
            runs.append((k, 1, pf))
    return steps, tuple(runs)


def _attn_kernel(it_ref, jt_ref, lamp_ref, q_ref, k_ref, v_ref, za_ref, hn_ref, o_ref,
                 m_sc, l_sc, acc_sc, s0_sc, s1_sc, *, t, pair_runs):
    s_bufs = (s0_sc, s1_sc)
    nt = (((1,), (1,)), ((), ()))

    lp = lamp_ref[...]
    d1 = jnp.sum(lp[0:1, :] * lp[1:2, :], axis=1, keepdims=True)
    d2 = jnp.sum(lp[2:3, :] * lp[3:4, :], axis=1, keepdims=True)
    lam = jnp.exp(d1) - jnp.exp(d2) + LAM_INIT_0

    def rows_of(tile, tiles=1):
        return pl.ds(pl.multiple_of(tile * t, t), tiles * t)

    def scores(n, width, s_ref):
        qs, ks = rows_of(it_ref[n]), rows_of(jt_ref[n], width)
        s_ref[0:t, 0:width * t] = lax.dot_general(
            q_ref[qs, :A_HEAD_DIM], k_ref[ks, :A_HEAD_DIM], nt,
            preferred_element_type=jnp.float32)
        s_ref[t:2 * t, 0:width * t] = lax.dot_general(
            q_ref[qs, A_HEAD_DIM:], k_ref[ks, A_HEAD_DIM:], nt,
            preferred_element_type=jnp.float32)

    def softmax_pv(n, s_ref, width, masked, first):
        i = it_ref[n]
        keys = width * t
        s = s_ref[:, 0:keys]
        if masked:
            rq = lax.broadcasted_iota(jnp.int32, (2 * t, 1), 0)
            rq = lax.shift_right_logical(jnp.where(rq >= t, rq - t, rq), CHUNK_SHIFT)
            ck = lax.shift_right_logical(
                lax.broadcasted_iota(jnp.int32, (1, keys), 1), CHUNK_SHIFT)
            s = jnp.where(ck <= rq, s, -jnp.inf)
        v = v_ref[rows_of(jt_ref[n], width), :]
        if first:
            m_new = jnp.broadcast_to(jnp.max(s, axis=1, keepdims=True), (2 * t, LANES))
            p = jnp.exp2(s - pltpu.repeat(m_new, keys // LANES, 1))
            l_sc[i] = jnp.broadcast_to(jnp.sum(p, axis=1, keepdims=True), (2 * t, LANES))
            m_sc[i] = m_new
            acc_sc[i] = jnp.dot(p.astype(jnp.bfloat16), v, preferred_element_type=jnp.float32)
            return
        m_prev = m_sc[i]
        m_new = jnp.maximum(m_prev, jnp.max(s, axis=1, keepdims=True))
        alpha = jnp.exp2(m_prev - m_new)
        p = jnp.exp2(s - pltpu.repeat(m_new, keys // LANES, 1))
        l_sc[i] = alpha * l_sc[i] + jnp.sum(p, axis=1, keepdims=True)
        m_sc[i] = m_new
        pv = jnp.dot(p.astype(jnp.bfloat16), v,
                     preferred_element_type=jnp.float32)
        acc_sc[i] = acc_sc[i] * pltpu.repeat(alpha, A_VDIM // LANES, 1) + pv

    def finish(i):
        inv_l = 1.0 / l_sc[i]
        acc = acc_sc[i] * pltpu.repeat(inv_l, A_VDIM // LANES, 1)
        o = acc[:t, :] - lam * acc[t:, :]
        ms = jnp.mean(o * o, axis=-1, keepdims=True)
        o = o * lax.rsqrt(ms + NORM_EPS) * hn_ref[...] * (1.0 - LAM_INIT_0)
        rows = rows_of(i)
        o_ref[rows, :] = (o * za_ref[rows, :].astype(jnp.float32)).astype(o_ref.dtype)

    def tick(n, par, width, masked, first, next_width):
        if next_width:
            scores(n + 1, next_width, s_bufs[1 - par])
        softmax_pv(n, s_bufs[par], width, masked, first)
        if masked:
            finish(it_ref[n])

    scores(0, pair_runs[0][2][0][0], s0_sc)
    for start, count, flags in pair_runs:
        def pair(pp, c, start=start, flags=flags):
            n = 2 * (start + pp)
            tick(n, 0, *flags[0])
            tick(n + 1, 1, *flags[1])
            return c
        if count == 1:
            pair(0, 0)
        else:
            lax.fori_loop(0, count, pair, 0)


def _attention(proj3, lamp, hn, *, t=512):
    b, s, _ = proj3.shape
    nblk = A_WIDTH // A_VDIM
    nq = s // t
    steps, pair_runs = _attn_schedule(nq)
    it = jnp.asarray([st[0] for st in steps], jnp.int32)
    jt = jnp.asarray([st[1] for st in steps], jnp.int32)
    head = lambda blk: pl.BlockSpec((None, s, A_VDIM),
                                    lambda bi, h, it_r, jt_r: (bi, 0, blk * nblk + h))
    grid_spec = pltpu.PrefetchScalarGridSpec(
        num_scalar_prefetch=2,
        grid=(b, A_HEADS),
        in_specs=[
            pl.BlockSpec((4, A_HEAD_DIM), lambda bi, h, it_r, jt_r: (0, 0)),
            head(0), head(1), head(2), head(3),
            pl.BlockSpec((1, A_VDIM), lambda bi, h, it_r, jt_r: (0, h)),
        ],
        out_specs=pl.BlockSpec((None, s, A_VDIM), lambda bi, h, it_r, jt_r: (bi, 0, h)),
        scratch_shapes=[
            pltpu.VMEM((nq, 2 * t, LANES), jnp.float32),
            pltpu.VMEM((nq, 2 * t, LANES), jnp.float32),
            pltpu.VMEM((nq, 2 * t, A_VDIM), jnp.float32),
            pltpu.VMEM((2 * t, ATT_WIDE * t), jnp.float32),
            pltpu.VMEM((2 * t, ATT_WIDE * t), jnp.float32),
        ],
    )
    return pl.pallas_call(
        functools.partial(_attn_kernel, t=t, pair_runs=pair_runs),
        grid_spec=grid_spec,
        out_shape=jax.ShapeDtypeStruct((b, s, A_WIDTH), jnp.bfloat16),
        compiler_params=pltpu.CompilerParams(
            dimension_semantics=("arbitrary", "arbitrary"),
            vmem_limit_bytes=VMEM_LIMIT),
        name="diff_attn",
    )(it, jt, lamp, proj3, proj3, proj3, proj3, hn)


def _out0_kernel(ya_ref, gu_ref, gv_ref, zb_ref, x_ref, wa_ref, wb_ref, g_ref,
                 lng_ref, lnb_ref, ws_ref, bst_ref, x1_ref, xn_ref, yb_sc, *, rows):
    cols = [slice(n * B_GROUP_DIM, (n + 1) * B_GROUP_DIM) for n in range(B_GROUPS)]
    lead = 2

    def attn_half(cs):
        x1_ref[:, cs] = x_ref[:, cs] + jnp.dot(ya_ref[...], wa_ref[:, cs],
                                              preferred_element_type=jnp.float32)

    for n in range(lead):
        attn_half(cols[n])
    v = gv_ref[...].astype(jnp.float32)
    mu = jnp.mean(v, axis=-1, keepdims=True)
    vc = v - mu
    var = jnp.mean(vc * vc, axis=-1, keepdims=True)
    rstd = lax.rsqrt(var + NORM_EPS)
    ci = lax.broadcasted_iota(jnp.int32, (B_SPAN, B_SPAN), 0) // CHUNK
    cj = lax.broadcasted_iota(jnp.int32, (B_SPAN, B_SPAN), 1) // CHUNK
    keep = ci >= cj
    for g, cs in enumerate(cols):
        vln = (vc[:, cs] * rstd * lng_ref[:, cs] + lnb_ref[:, cs]).astype(jnp.bfloat16)
        w = jnp.where(keep, ws_ref[g], 0.0).astype(jnp.bfloat16)
        bcol = bst_ref[:, g:g + 1]
        for sp in range(rows // B_SPAN):
            rs = slice(sp * B_SPAN, (sp + 1) * B_SPAN)
            mixed = jnp.dot(w, vln[rs, :], preferred_element_type=jnp.float32) + bcol
            u = gu_ref[rs, cs].astype(jnp.float32)
            z = zb_ref[rs, cs].astype(jnp.float32)
            yb_sc[rs, cs] = (u * mixed * z).astype(yb_sc.dtype)
        if g + lead < B_GROUPS:
            attn_half(cols[g + lead])
    ssq = jnp.zeros((rows, 1), jnp.float32)
    for cs in cols:
        x1 = x1_ref[:, cs] + jnp.dot(yb_sc[...], wb_ref[:, cs],
                                     preferred_element_type=jnp.float32)
        x1_ref[:, cs] = x1
        ssq = ssq + jnp.sum(x1 * x1, axis=-1, keepdims=True)
    scale = lax.rsqrt(ssq * (1.0 / D_MODEL) + NORM_EPS)
    xn_ref[...] = (x1_ref[...] * scale * g_ref[...]).astype(xn_ref.dtype)


def _out0(ya, proj, x2, w_bf, g, ln_g, ln_b, w_s, b_st, *, tm=256):
    t = x2.shape[0]
    const = pl.Buffered(1)
    base = 4 * A_WIDTH // B_WIDTH
    vec = lambda n: pl.BlockSpec((1, n), lambda i: (0, 0))
    return pl.pallas_call(
        functools.partial(_out0_kernel, rows=tm),
        grid=(t // tm,),
        in_specs=[
            pl.BlockSpec((tm, A_WIDTH), lambda i: (i, 0)),
            pl.BlockSpec((tm, B_WIDTH), lambda i: (i, base)),
            pl.BlockSpec((tm, B_WIDTH), lambda i: (i, base + 1)),
            pl.BlockSpec((tm, B_WIDTH), lambda i: (i, base + 2)),
            pl.BlockSpec((tm, D_MODEL), lambda i: (i, 0)),
            pl.BlockSpec((A_WIDTH, D_MODEL), lambda i: (0, 0), pipeline_mode=const),
            pl.BlockSpec((B_WIDTH, D_MODEL), lambda i: (1, 0), pipeline_mode=const),
            vec(D_MODEL), vec(B_WIDTH), vec(B_WIDTH),
            pl.BlockSpec((B_GROUPS, B_SPAN, B_SPAN), lambda i: (0, 0, 0)),
            pl.BlockSpec((B_SPAN, B_GROUPS), lambda i: (0, 0)),
        ],
        out_specs=[
            pl.BlockSpec((tm, D_MODEL), lambda i: (i, 0)),
            pl.BlockSpec((tm, D_MODEL), lambda i: (i, 0)),
        ],
        out_shape=[
            jax.ShapeDtypeStruct((t, D_MODEL), jnp.float32),
            jax.ShapeDtypeStruct((t, D_MODEL), jnp.bfloat16),
        ],
        scratch_shapes=[pltpu.VMEM((tm, B_WIDTH), jnp.bfloat16)],
        compiler_params=pltpu.CompilerParams(
            dimension_semantics=("arbitrary",),
            vmem_limit_bytes=VMEM_LIMIT),
        name="out0",
    )(ya, proj, proj, proj, x2, w_bf, w_bf, g, ln_g, ln_b, w_s, b_st)


def _proj1_kernel(xn_ref, w_ref, o_ref):
    def proj():
        return jnp.dot(xn_ref[...], w_ref[...], preferred_element_type=jnp.float32)

    @pl.when(pl.program_id(0) == 0)
    def _():
        o_ref[...] = proj().astype(o_ref.dtype)

    @pl.when(pl.program_id(0) == 1)
    def _():
        o_ref[...] = _silu(proj()).astype(o_ref.dtype)


def _proj1(xn, w_bf, *, tm=512):
    t = xn.shape[0]
    return pl.pallas_call(
        _proj1_kernel,
        grid=(2, t // tm),
        in_specs=[
            pl.BlockSpec((tm, D_MODEL), lambda j, i: (i, 0)),
            pl.BlockSpec((D_MODEL, C_WIDTH), lambda j, i: (0, j)),
        ],
        out_specs=pl.BlockSpec((tm, C_WIDTH), lambda j, i: (i, j)),
        out_shape=jax.ShapeDtypeStruct((t, 2 * C_WIDTH), jnp.bfloat16),
        compiler_params=pltpu.CompilerParams(
            dimension_semantics=("arbitrary", "arbitrary"),
            vmem_limit_bytes=VMEM_LIMIT),
        name="proj1",
    )(xn, w_bf)


def _segment_pitch(seg):
    units = seg // SUBLANES
    return SUBLANES * (units + 1 if units % 2 == 0 else units)


def _build_gate_slabs(wp_ref, wg_sc, tmp_sc):
    row = lax.broadcasted_iota(jnp.int32, (GATE_PAD, LANES), 0)
    col = lax.broadcasted_iota(jnp.int32, (GATE_PAD, LANES), 1)
    for j in range(N_CT):
        tmp_sc[...] = jnp.zeros(tmp_sc.shape, jnp.float32)
        h_lo = (LANES * j) // C_BLOCK
        h_hi = (LANES * j + LANES - 1) // C_BLOCK
        for gate in range(2):
            for h in range(h_lo, h_hi + 1):
                sel = jnp.where(row == col + (LANES * j - C_BLOCK * h), 1.0, 0.0)
                part = jnp.dot(wp_ref[gate, h], sel.astype(jnp.bfloat16),
                               preferred_element_type=jnp.float32)
                r0 = C_BLOCK * h - GATE_WS[j]
                cs = slice(gate * LANES, (gate + 1) * LANES)
                tmp_sc[r0:r0 + C_BLOCK, cs] = tmp_sc[r0:r0 + C_BLOCK, cs] + part[:C_BLOCK, :]
        wg_sc[j] = (0.5 * tmp_sc[...]).astype(jnp.bfloat16)


def _rglru_kernel(xb_ref, zc_ref, cw_ref, cb_ref, wp_ref, ba_ref, bx_ref,
                  lam_ref, o_ref, pad_sc, xc_sc, xcb_sc, ga_sc, gx_sc, h_sc,
                  wg_ref, tmp_sc, hl_sc, p_sc, prm_sc, *, ts, rows):
    si = pl.program_id(1)
    halo = SUBLANES
    vshape = (N_CT, SUBLANES, LANES)

    @pl.when((pl.program_id(0) == 0) & (si == 0))
    def _():
        _build_gate_slabs(wp_ref, wg_ref, tmp_sc)
        lam = lam_ref[...]
        sp = jnp.maximum(-lam, 0.0) + jnp.log1p(jnp.exp(-jnp.abs(lam)))
        prm_sc[0] = jnp.broadcast_to((-0.5 * C_GATE_C) * sp, vshape)
        prm_sc[1] = jnp.broadcast_to(0.5 * ba_ref[...], vshape)
        prm_sc[2] = jnp.broadcast_to(0.5 * bx_ref[...], vshape)

    @pl.when(si == 0)
    def _():
        pad_sc[0:halo, :] = jnp.zeros((halo, C_WIDTH), jnp.float32)
        h_sc[...] = jnp.zeros(h_sc.shape, jnp.float32)

    @pl.when(si > 0)
    def _():
        pad_sc[0:halo, :] = pad_sc[ts:ts + halo, :]

    seg = ts // SUBLANES
    pitch = _segment_pitch(seg)

    def put(dst_sc, j, val):
        for s in range(SUBLANES):
            dst_sc[j, s * pitch:s * pitch + seg, :] = val[s * seg:(s + 1) * seg, :]

    pad_sc[halo:halo + ts, :] = xb_ref[...].astype(jnp.float32)
    xc = cb_ref[...] + cw_ref[C_CONV - 1:C_CONV, :] * pad_sc[halo:halo + ts, :]
    for tap in range(C_CONV - 1):
        off = halo - (C_CONV - 1) + tap
        xc = xc + cw_ref[tap:tap + 1, :] * pad_sc[off:off + ts, :]
    xcb_sc[...] = xc.astype(jnp.bfloat16)
    for j in range(N_CT):
        put(xc_sc, j, xc[:, j * LANES:(j + 1) * LANES])

    for j in range(N_CT):
        ws = GATE_WS[j]
        g = jnp.dot(xcb_sc[:, ws:ws + GATE_WIN], wg_ref[j],
                    preferred_element_type=jnp.float32)
        put(ga_sc, j, g[:, :LANES])
        put(gx_sc, j, g[:, LANES:])

    blocks_per_seg = seg // rows

    def padded_rows(it):
        start = (it // blocks_per_seg) * pitch + (it % blocks_per_seg) * rows
        return pl.ds(pl.multiple_of(start, SUBLANES), rows)

    def local(i, carry):
        h, p = carry
        rs = pl.ds(i, SUBLANES, stride=pitch)
        half_c = prm_sc[0]
        tr = jnp.tanh(ga_sc[:, rs, :] + prm_sc[1])
        ti = jnp.tanh(gx_sc[:, rs, :] + prm_sc[2])
        log_a = half_c + half_c * tr
        a = jnp.exp(log_a)
        half_a = 0.5 * a
        y = (-0.25 - half_a * half_a) * jnp.tanh(log_a)
        half_beta = jnp.where(y > 0.0, y * lax.rsqrt(y), 0.0)
        x = xc_sc[:, rs, :]
        b = half_beta * (x + x * ti)
        h = a * h + b
        p = a * p
        blk = pl.ds(pl.multiple_of(i * SUBLANES, SUBLANES), SUBLANES)
        hl_sc[:, blk, :] = h
        p_sc[:, blk, :] = p
        return h, p

    h_end, p_end = lax.fori_loop(
        0, seg, local,
        (jnp.zeros(vshape, jnp.float32), jnp.ones(vshape, jnp.float32)), unroll=2)

    sub = lax.broadcasted_iota(jnp.int32, vshape, 1)
    a, b = p_end, h_end
    for d in (1, 2, 4):
        keep = sub >= d
        a_sh = pltpu.roll(a, d, 1)
        b_sh = pltpu.roll(b, d, 1)
        b = jnp.where(keep, a * b_sh + b, b)
        a = jnp.where(keep, a * a_sh, a)
    carry_in = h_sc[...]
    seg_end = a * carry_in + b
    h_in = jnp.where(sub == 0, carry_in, pltpu.roll(seg_end, 1, 1))
    h_sc[...] = jnp.broadcast_to(seg_end[:, SUBLANES - 1:SUBLANES, :], vshape)

    def fix(i, c):
        blk = pl.ds(pl.multiple_of(i * SUBLANES, SUBLANES), SUBLANES)
        xc_sc[:, pl.ds(i, SUBLANES, stride=pitch), :] = hl_sc[:, blk, :] + p_sc[:, blk, :] * h_in
        return c

    lax.fori_loop(0, seg, fix, 0)

    def emit(it, c):
        rs = pl.ds(pl.multiple_of(it * rows, rows), rows)
        ps = padded_rows(it)
        h = jnp.concatenate([xc_sc[j, ps, :] for j in range(N_CT)], axis=-1)
        o_ref[rs, :] = (h * zc_ref[rs, :].astype(jnp.float32)).astype(o_ref.dtype)
        return c

    lax.fori_loop(0, ts // rows, emit, 0)


def _rglru(xbz3, cw, cb, wp, ba, bx, lam, *, ts=256, rows=16):
    b, s, _ = xbz3.shape
    const = pl.Buffered(1)
    vec = lambda: pl.BlockSpec((N_CT, 1, LANES), lambda bi, si: (0, 0, 0))
    tiled = lambda: pltpu.VMEM((N_CT, ts, LANES), jnp.float32)
    padded = lambda: pltpu.VMEM(
        (N_CT, SUBLANES * _segment_pitch(ts // SUBLANES), LANES), jnp.float32)
    tile_major = lambda v: v.reshape(N_CT, 1, LANES)
    return pl.pallas_call(
        functools.partial(_rglru_kernel, ts=ts, rows=rows),
        grid=(b, s // ts),
        in_specs=[
            pl.BlockSpec((None, ts, C_WIDTH), lambda bi, si: (bi, si, 0)),
            pl.BlockSpec((None, ts, C_WIDTH), lambda bi, si: (bi, si, 1)),
            pl.BlockSpec((C_CONV, C_WIDTH), lambda bi, si: (0, 0)),
            pl.BlockSpec((1, C_WIDTH), lambda bi, si: (0, 0)),
            pl.BlockSpec((2, C_HEADS, GATE_PAD, GATE_PAD), lambda bi, si: (0, 0, 0, 0),
                         pipeline_mode=const),
            vec(), vec(), vec(),
        ],
        out_specs=pl.BlockSpec((None, ts, C_WIDTH), lambda bi, si: (bi, si, 0)),
        out_shape=jax.ShapeDtypeStruct((b, s, C_WIDTH), jnp.bfloat16),
        scratch_shapes=[
            pltpu.VMEM((ts + SUBLANES, C_WIDTH), jnp.float32),
            padded(),
            pltpu.VMEM((ts, C_WIDTH), jnp.bfloat16),
            padded(),
            padded(),
            pltpu.VMEM((N_CT, SUBLANES, LANES), jnp.float32),
            pltpu.VMEM((N_CT, GATE_WIN, 2 * LANES), jnp.bfloat16),
            pltpu.VMEM((GATE_WIN, 2 * LANES), jnp.float32),
            tiled(),
            tiled(),
            pltpu.VMEM((3, N_CT, SUBLANES, LANES), jnp.float32),
        ],
        compiler_params=pltpu.CompilerParams(
            dimension_semantics=("arbitrary", "arbitrary"),
            vmem_limit_bytes=VMEM_LIMIT),
        name="rglru",
    )(xbz3, xbz3, cw, cb, wp, tile_major(ba), tile_major(bx), tile_major(lam))


def _out1_kernel(hz_ref, x_ref, w_ref, g_ref, o_ref):
    y = jnp.dot(hz_ref[...], w_ref[...], preferred_element_type=jnp.float32)
    o_ref[...] = _rms_rows(x_ref[...] + y, g_ref[...])


def _out1(hz, x1, w_bf, g, *, tm=256):
    t = x1.shape[0]
    return pl.pallas_call(
        _out1_kernel,
        grid=(t // tm,),
        in_specs=[
            pl.BlockSpec((tm, C_WIDTH), lambda i: (i, 0)),
            pl.BlockSpec((tm, D_MODEL), lambda i: (i, 0)),
            pl.BlockSpec((C_WIDTH, D_MODEL), lambda i: (0, 0),
                         pipeline_mode=pl.Buffered(1)),
            pl.BlockSpec((1, D_MODEL), lambda i: (0, 0)),
        ],
        out_specs=pl.BlockSpec((tm, D_MODEL), lambda i: (i, 0)),
        out_shape=jax.ShapeDtypeStruct((t, D_MODEL), jnp.float32),
        compiler_params=pltpu.CompilerParams(
            dimension_semantics=("arbitrary",),
            vmem_limit_bytes=VMEM_LIMIT),
        name="out1",
    )(hz, x1, w_bf, g)


def kernel(x, ab_norm, ab_w_in, ab_lambda, ab_head_norm, ab_sgu_ln_g, ab_sgu_ln_b, ab_sgu_w, ab_sgu_b, ab_w_out, c_norm, c_w_in, c_conv_w, c_conv_b, c_gate_a_w, c_gate_a_b, c_gate_x_w, c_gate_x_b, c_lambda, c_w_out, final_norm):
    bsz, seq, d = x.shape
    t = bsz * seq
    bf = jnp.bfloat16
    row = lambda v: v.reshape(1, -1)
    x2 = x.reshape(t, d)

    proj, (w_out0, w_in1, w_out1) = _proj0(
        x2, row(ab_norm[0]), ab_w_in[0], [ab_w_out[0], c_w_in[0], c_w_out[0]])
    ya = _attention(proj.reshape(bsz, seq, AB_IN), ab_lambda[0], row(ab_head_norm[0]))
    x1, xn1 = _out0(ya.reshape(t, A_WIDTH), proj, x2, w_out0, row(c_norm[0]),
                    row(ab_sgu_ln_g[0]), row(ab_sgu_ln_b[0]), ab_sgu_w[0], ab_sgu_b[0].T)

    xbz = _proj1(xn1, w_in1)
    padw = GATE_PAD - C_BLOCK
    wp = jnp.pad(jnp.stack([c_gate_a_w[0], c_gate_x_w[0]]).astype(bf),
                 ((0, 0), (0, 0), (0, padw), (0, padw)))
    hz = _rglru(xbz.reshape(bsz, seq, 2 * C_WIDTH), c_conv_w[0], row(c_conv_b[0]), wp,
                row(c_gate_a_b[0]), row(c_gate_x_b[0]), row(c_lambda[0]))
    out = _out1(hz.reshape(t, C_WIDTH), x1, w_out1, row(final_norm))
    return out.reshape(bsz, seq, d)
```
